```python
import jax, jax.numpy as jnp
from jax import lax
import numpy as np

D_MODEL = 1024
BATCH = 1
SEQ = 16384
DEPTH = 2

N_MIXERS = 2
ROPE_THETA = 500000.0
Q_BLOCK = 128
NEG_INF = -1e30
LN_EPS = 1e-5
RMS_EPS = 1e-6
MLA_HEADS = 8
MLA_Q_RANK = 384
MLA_KV_RANK = 256
MLA_NOPE_DIM = 128
MLA_ROPE_DIM = 64
MLA_V_DIM = 128
MLA_QK_DIM = MLA_NOPE_DIM + MLA_ROPE_DIM
MOBA_HEADS = 8
MOBA_HEAD_DIM = D_MODEL // MOBA_HEADS
MOBA_ROT_DIM = MOBA_HEAD_DIM // 4
MOBA_BLOCK = 256
MOBA_TOPK = 3
MOBA_Q_CHUNK = 32
D_FF = -(-(8 * D_MODEL) // (3 * 256)) * 256
DEEPNORM_ALPHA = (2 * DEPTH) ** 0.25
DEEPNORM_BETA = (8 * DEPTH) ** -0.25
N_MLA_LAYERS = (DEPTH + 1) // 2
N_MOBA_LAYERS = DEPTH // 2

kernel_name = "hybrid_mla_moba_deepnorm"


def layer_norm(x, g, b):
    xf = x.astype(jnp.float32)
    mu = xf.mean(-1, keepdims=True)
    var = jnp.square(xf - mu).mean(-1, keepdims=True)
    return ((xf - mu) * lax.rsqrt(var + LN_EPS) * g + b).astype(x.dtype)


def rms_norm(x, g):
    xf = x.astype(jnp.float32)
    return (xf * lax.rsqrt(jnp.square(xf).mean(-1, keepdims=True) + RMS_EPS) * g).astype(x.dtype)


def rotary_tables(seq_len, rot_dim):
    inv_freq = ROPE_THETA ** (-jnp.arange(0, rot_dim, 2, dtype=jnp.float32) / rot_dim)
    ang = jnp.arange(seq_len, dtype=jnp.float32)[:, None] * inv_freq[None, :]
    return jnp.cos(ang), jnp.sin(ang)


def apply_rope(x, cos, sin):
    x1, x2 = jnp.split(x, 2, axis=-1)
    out = jnp.concatenate([x1 * cos - x2 * sin, x2 * cos + x1 * sin], axis=-1)
    return out.astype(x.dtype)


def partial_rope(x, cos, sin):
    return jnp.concatenate([apply_rope(x[..., :MOBA_ROT_DIM], cos, sin), x[..., MOBA_ROT_DIM:]], axis=-1)


def mla_mixer(x, cos, sin, w_dqkv, q_norm, w_uq, kv_norm, w_ukv, w_o):
    B, S, _ = x.shape
    H = MLA_HEADS
    lat = x @ w_dqkv
    c_q, c_kv, k_rope = jnp.split(lat, [MLA_Q_RANK, MLA_Q_RANK + MLA_KV_RANK], axis=-1)
    q = (rms_norm(c_q, q_norm) @ w_uq).reshape(B, S, H, MLA_QK_DIM).transpose(0, 2, 1, 3)
    q_nope = q[..., :MLA_NOPE_DIM]
    q_rope = apply_rope(q[..., MLA_NOPE_DIM:], cos, sin)
    k_rope = apply_rope(k_rope, cos, sin)
    kv = (rms_norm(c_kv, kv_norm) @ w_ukv).reshape(B, S, H, MLA_NOPE_DIM + MLA_V_DIM).transpose(0, 2, 1, 3)
    k_nope, v = kv[..., :MLA_NOPE_DIM], kv[..., MLA_NOPE_DIM:]
    scale = MLA_QK_DIM ** -0.5
    kpos = jnp.arange(S)

    def attend_block(i):
        start = i * Q_BLOCK
        qn = lax.dynamic_slice_in_dim(q_nope, start, Q_BLOCK, axis=2)
        qr = lax.dynamic_slice_in_dim(q_rope, start, Q_BLOCK, axis=2)
        s = (jnp.einsum('bhqd,bhkd->bhqk', qn, k_nope)
             + jnp.einsum('bhqr,bkr->bhqk', qr, k_rope)).astype(jnp.float32) * scale
        qpos = start + jnp.arange(Q_BLOCK)
        s = jnp.where(kpos[None, :] <= qpos[:, None], s, NEG_INF)
        p = jax.nn.softmax(s, axis=-1).astype(v.dtype)
        return jnp.einsum('bhqk,bhkv->bhqv', p, v)

    out = lax.map(attend_block, jnp.arange(S // Q_BLOCK))
    out = out.transpose(1, 0, 3, 2, 4).reshape(B, S, H * MLA_V_DIM)
    return out @ w_o


def moba_mixer(x, cos, sin, w_qkv, w_o):
    B, S, _ = x.shape
    H, Dh = MOBA_HEADS, MOBA_HEAD_DIM
    qkv = (x @ w_qkv).reshape(B, S, 3, H, Dh).transpose(2, 0, 3, 1, 4)
    q = partial_rope(qkv[0], cos, sin)
    k = partial_rope(qkv[1], cos, sin)
    v = qkv[2]
    nb = max(-(-S // MOBA_BLOCK), MOBA_TOPK)
    pad = nb * MOBA_BLOCK - S
    k_blocks = jnp.pad(k, ((0, 0), (0, 0), (0, pad), (0, 0))).reshape(B, H, nb, MOBA_BLOCK, Dh)
    v_blocks = jnp.pad(v, ((0, 0), (0, 0), (0, pad), (0, 0))).reshape(B, H, nb, MOBA_BLOCK, Dh)
    k_mean = k_blocks.astype(jnp.float32).mean(axis=3).astype(k.dtype)
    scale = Dh ** -0.5
    bi = jnp.arange(B)[:, None, None, None]
    hi = jnp.arange(H)[None, :, None, None]
    blk_ids = jnp.arange(nb)
    offs = jnp.arange(MOBA_BLOCK)
    n_sel = MOBA_TOPK * MOBA_BLOCK

    def attend_chunk(i):
        start = i * MOBA_Q_CHUNK
        qc = lax.dynamic_slice_in_dim(q, start, MOBA_Q_CHUNK, axis=2)
        qpos = start + jnp.arange(MOBA_Q_CHUNK)
        own = start // MOBA_BLOCK
        gate = jnp.einsum('bhqd,bhnd->bhqn', qc, k_mean).astype(jnp.float32)
        gate = jnp.where(blk_ids < own, gate, NEG_INF)
        _, sel = lax.top_k(gate, MOBA_TOPK)
        sel_valid = sel < own
        k_sel = k_blocks[bi, hi, sel]
        v_sel = v_blocks[bi, hi, sel]
        s_sel = jnp.einsum('bhqd,bhqnkd->bhqnk', qc, k_sel).astype(jnp.float32) * scale
        s_sel = jnp.where(sel_valid[..., None], s_sel, NEG_INF).reshape(B, H, MOBA_Q_CHUNK, n_sel)
        k_own = lax.dynamic_index_in_dim(k_blocks, own, axis=2, keepdims=False)
        v_own = lax.dynamic_index_in_dim(v_blocks, own, axis=2, keepdims=False)
        s_own = jnp.einsum('bhqd,bhkd->bhqk', qc, k_own).astype(jnp.float32) * scale
        kpos = own * MOBA_BLOCK + offs
        s_own = jnp.where(kpos[None, :] <= qpos[:, None], s_own, NEG_INF)
        p = jax.nn.softmax(jnp.concatenate([s_sel, s_own], axis=-1), axis=-1).astype(v.dtype)
        p_sel = p[..., :n_sel].reshape(B, H, MOBA_Q_CHUNK, MOBA_TOPK, MOBA_BLOCK)
        p_own = p[..., n_sel:]
        return (jnp.einsum('bhqnk,bhqnkd->bhqd', p_sel, v_sel)
                + jnp.einsum('bhqk,bhkd->bhqd', p_own, v_own))

    out = lax.map(attend_chunk, jnp.arange(S // MOBA_Q_CHUNK))
    out = out.transpose(1, 0, 3, 2, 4).reshape(B, S, H * Dh)
    return out @ w_o


def swiglu(x, w_in, w_out):
    g, u = jnp.split(x @ w_in, 2, axis=-1)
    return (jax.nn.silu(g) * u) @ w_out


def setup_inputs(seed: int = 0) -> dict:
    key = jax.random.key(seed)
    ks = jax.random.split(key, 20)
    nrm = lambda k, shape, fan_in, gain=1.0: jax.random.normal(k, shape, jnp.float32) * (fan_in ** -0.5) * gain
    gain_vec = lambda k, shape: 1.0 + 0.02 * jax.random.normal(k, shape, jnp.float32)
    Lm, Lb, L = N_MLA_LAYERS, N_MOBA_LAYERS, DEPTH
    return {
        "x": jax.random.normal(ks[0], (BATCH, SEQ, D_MODEL), jnp.float32),
        "mla_w_dqkv": nrm(ks[1], (Lm, D_MODEL, MLA_Q_RANK + MLA_KV_RANK + MLA_ROPE_DIM), D_MODEL),
        "mla_q_norm": gain_vec(ks[2], (Lm, MLA_Q_RANK)),
        "mla_w_uq": nrm(ks[3], (Lm, MLA_Q_RANK, MLA_HEADS * MLA_QK_DIM), MLA_Q_RANK),
        "mla_kv_norm": gain_vec(ks[4], (Lm, MLA_KV_RANK)),
        "mla_w_ukv": nrm(ks[5], (Lm, MLA_KV_RANK, MLA_HEADS * (MLA_NOPE_DIM + MLA_V_DIM)), MLA_KV_RANK),
        "mla_w_o": nrm(ks[6], (Lm, MLA_HEADS * MLA_V_DIM, D_MODEL), MLA_HEADS * MLA_V_DIM, DEEPNORM_BETA),
        "moba_w_qkv": nrm(ks[7], (Lb, D_MODEL, 3 * MOBA_HEADS * MOBA_HEAD_DIM), D_MODEL),
        "moba_w_o": nrm(ks[8], (Lb, MOBA_HEADS * MOBA_HEAD_DIM, D_MODEL), MOBA_HEADS * MOBA_HEAD_DIM, DEEPNORM_BETA),
        "ffn_w_in": nrm(ks[9], (L, D_MODEL, 2 * D_FF), D_MODEL),
        "ffn_w_out": nrm(ks[10], (L, D_FF, D_MODEL), D_FF, DEEPNORM_BETA),
        "ln_mix_g": gain_vec(ks[11], (L, D_MODEL)),
        "ln_mix_b": 0.02 * jax.random.normal(ks[12], (L, D_MODEL), jnp.float32),
        "ln_ffn_g": gain_vec(ks[13], (L, D_MODEL)),
        "ln_ffn_b": 0.02 * jax.random.normal(ks[14], (L, D_MODEL), jnp.float32),
    }


def reference(x, mla_w_dqkv, mla_q_norm, mla_w_uq, mla_kv_norm, mla_w_ukv, mla_w_o,
              moba_w_qkv, moba_w_o, ffn_w_in, ffn_w_out, ln_mix_g, ln_mix_b, ln_ffn_g, ln_ffn_b):
    S = x.shape[1]
    cos_mla, sin_mla = rotary_tables(S, MLA_ROPE_DIM)
    cos_moba, sin_moba = rotary_tables(S, MOBA_ROT_DIM)
    for i in range(DEPTH):
        j = i // N_MIXERS
        if i % N_MIXERS == 0:
            h = mla_mixer(x, cos_mla, sin_mla, mla_w_dqkv[j], mla_q_norm[j], mla_w_uq[j],
                          mla_kv_norm[j], mla_w_ukv[j], mla_w_o[j])
        else:
            h = moba_mixer(x, cos_moba, sin_moba, moba_w_qkv[j], moba_w_o[j])
        x = layer_norm(DEEPNORM_ALPHA * x + h, ln_mix_g[i], ln_mix_b[i])
        x = layer_norm(DEEPNORM_ALPHA * x + swiglu(x, ffn_w_in[i], ffn_w_out[i]), ln_ffn_g[i], ln_ffn_b[i])
    return x
```

```python
import functools
import math

import jax
import jax.numpy as jnp
from jax import lax
from jax.experimental import pallas as pl
from jax.experimental.pallas import tpu as pltpu

D_MODEL = 1024
DEPTH = 2
ROPE_THETA = 500000.0
NEG_INF = -1e30
LN_EPS = 1e-5
RMS_EPS = 1e-6
MLA_HEADS = 8
MLA_Q_RANK = 384
MLA_KV_RANK = 256
MLA_NOPE_DIM = 128
MLA_ROPE_DIM = 64
MLA_V_DIM = 128
MLA_QK_DIM = MLA_NOPE_DIM + MLA_ROPE_DIM
MOBA_HEADS = 8
MOBA_HEAD_DIM = D_MODEL // MOBA_HEADS
MOBA_ROT_DIM = MOBA_HEAD_DIM // 4
MOBA_BLOCK = 256
MOBA_TOPK = 3
D_FF = 2816
DEEPNORM_ALPHA = (2 * DEPTH) ** 0.25

LANES = 128
MXU_DIM = 256
LOG2E = math.log2(math.e)

ROW_TILE = 256
MLA_ATTN_TILE = 512
MOBA_KEY_CHUNK = 4 * MOBA_BLOCK
FF_CHUNKS = (768, 768, 768, 512)
VMEM_LIMIT = 48 * 1024 * 1024

BF16 = jnp.bfloat16
F32 = jnp.float32


def _dot(a, b):
    return jnp.dot(a, b, preferred_element_type=F32)


def _dot_nt(a, b, precision=None):
    return lax.dot_general(a, b, (((1,), (1,)), ((), ())), preferred_element_type=F32, precision=precision)


def _resident(shape, index_map):
    return pl.BlockSpec(shape, index_map, pipeline_mode=pl.Buffered(1))


def _params():
    return pltpu.CompilerParams(dimension_semantics=("arbitrary",) * 2, vmem_limit_bytes=VMEM_LIMIT)


def _params1():
    return pltpu.CompilerParams(dimension_semantics=("arbitrary",), vmem_limit_bytes=VMEM_LIMIT)


def _rms(x, g):
    return x * lax.rsqrt(jnp.mean(x * x, axis=-1, keepdims=True) + RMS_EPS) * g


def _mla_proj_kernel(x_ref, wd_ref, qn_ref, wq_ref, kvn_ref, wkv_ref, tc_ref, ts_ref,
                     q_ref, k_ref, v_ref):
    H = MLA_HEADS
    qscale = (MLA_QK_DIM ** -0.5) * LOG2E
    xb = x_ref[...].astype(BF16)
    lat = _dot(xb, wd_ref[...])
    o1 = MLA_Q_RANK
    o2 = o1 + MLA_KV_RANK
    cq = _rms(lat[:, :o1], qn_ref[...]).astype(BF16)
    ckv = _rms(lat[:, o1:o2], kvn_ref[...]).astype(BF16)
    tc = tc_ref[...]
    ts = ts_ref[...]
    kr = lat[:, o2:o2 + LANES] * tc + lat[:, o2 + LANES:o2 + 2 * LANES] * ts
    lane = lax.broadcasted_iota(jnp.int32, kr.shape, 1)
    kr_lo = jnp.where(lane < MLA_ROPE_DIM, kr, 0.0).astype(BF16)
    kr_hi = jnp.where(lane >= MLA_ROPE_DIM, kr, 0.0).astype(BF16)
    q = _dot(cq, wq_ref[...])
    kv = _dot(ckv, wkv_ref[...])
    r0 = H * MLA_NOPE_DIM
    r1 = r0 + H * MLA_ROPE_DIM
    for p in range(H // 2):
        qr = (q[:, r0 + LANES * p:r0 + LANES * (p + 1)] * tc
              + q[:, r1 + LANES * p:r1 + LANES * (p + 1)] * ts)
        qr = (qr * qscale).astype(BF16)
        for h in (2 * p, 2 * p + 1):
            c = 2 * LANES * h
            q_ref[:, c:c + LANES] = (q[:, LANES * h:LANES * (h + 1)] * qscale).astype(BF16)
            q_ref[:, c + LANES:c + 2 * LANES] = qr
            k_ref[:, c:c + LANES] = kv[:, c:c + LANES].astype(BF16)
            k_ref[:, c + LANES:c + 2 * LANES] = kr_lo if h % 2 == 0 else kr_hi
            v_ref[:, LANES * h:LANES * (h + 1)] = kv[:, c + LANES:c + 2 * LANES].astype(BF16)


def _mla_proj(x, wd, qn, wq, kvn, wkv, tc, ts):
    S = x.shape[0]
    tm = ROW_TILE
    H = MLA_HEADS
    row = lambda w: pl.BlockSpec((tm, w), lambda i: (i, 0))
    full = lambda a: _resident(a.shape, lambda i: (0, 0))
    return pl.pallas_call(
        _mla_proj_kernel,
        grid=(S // tm,),
        in_specs=[row(D_MODEL), full(wd), full(qn), full(wq), full(kvn), full(wkv), row(LANES), row(LANES)],
        out_specs=[row(2 * LANES * H), row(2 * LANES * H), row(LANES * H)],
        out_shape=[jax.ShapeDtypeStruct((S, 2 * LANES * H), BF16),
                   jax.ShapeDtypeStruct((S, 2 * LANES * H), BF16),
                   jax.ShapeDtypeStruct((S, LANES * H), BF16)],
        compiler_params=_params1(),
        name="mla_proj",
    )(x, wd, qn, wq, kvn, wkv, tc, ts)


def _attn_kernel(*refs, tq, tk, moba):
    q_ref, k_ref, v_ref, o_ref, m_ref, l_ref, acc_ref = refs
    i = pl.program_id(1)
    q = q_ref[...]

    d0 = pl.multiple_of(i * tq, tq)
    if moba:
        s = _dot_nt(q_ref[:, :LANES], k_ref[pl.ds(d0, tq), :LANES])
    else:
        s = _dot_nt(q, k_ref[pl.ds(d0, tq), :])
    row = lax.broadcasted_iota(jnp.int32, s.shape, 0)
    col = lax.broadcasted_iota(jnp.int32, s.shape, 1)
    s = jnp.where(col <= row, s, NEG_INF)
    m0 = jnp.max(s, axis=-1, keepdims=True)
    p = jnp.exp2(s - m0)
    m_ref[...] = m0
    l_ref[...] = jnp.sum(p, axis=-1, keepdims=True)
    acc_ref[...] = _dot(p.astype(BF16), v_ref[pl.ds(d0, tq), :])

    def body(j, carry):
        c0 = pl.multiple_of(j * tk, tk)
        sc = _dot_nt(q, k_ref[pl.ds(c0, tk), :])
        m_prev = m_ref[...]
        m_new = jnp.maximum(m_prev, jnp.max(sc, axis=-1, keepdims=True))
        alpha = jnp.exp2(m_prev - m_new)
        pc = jnp.exp2(sc - m_new)
        l_ref[...] = alpha * l_ref[...] + jnp.sum(pc, axis=-1, keepdims=True)
        acc_ref[...] = alpha * acc_ref[...] + _dot(pc.astype(BF16), v_ref[pl.ds(c0, tk), :])
        m_ref[...] = m_new
        return carry

    blocks_per_chunk = tk // tq
    n_past = (i + blocks_per_chunk - 1) // blocks_per_chunk
    lax.fori_loop(0, n_past, body, 0)
    o_ref[...] = (acc_ref[...] / l_ref[...]).astype(o_ref.dtype)


def _attention(q, k, v, *, heads, tq, tk, moba):
    S = q.shape[0]
    in_specs = [pl.BlockSpec((tq, 2 * LANES), lambda h, i: (i, h)),
                _resident((S, 2 * LANES), lambda h, i: (0, h)),
                _resident((S, LANES), lambda h, i: (0, h))]
    return pl.pallas_call(
        functools.partial(_attn_kernel, tq=tq, tk=tk, moba=moba),
        grid=(heads, S // tq),
        in_specs=in_specs,
        out_specs=pl.BlockSpec((tq, LANES), lambda h, i: (i, h)),
        out_shape=jax.ShapeDtypeStruct((S, LANES * heads), BF16),
        scratch_shapes=[pltpu.VMEM((tq, 1), F32), pltpu.VMEM((tq, 1), F32), pltpu.VMEM((tq, LANES), F32)],
        compiler_params=_params(),
        name="moba_attn" if moba else "mla_attn",
    )(q, k, v)


def _layer_norm(z, g, b):
    mu = jnp.mean(z, axis=-1, keepdims=True)
    d = z - mu
    var = jnp.mean(d * d, axis=-1, keepdims=True)
    return d * lax.rsqrt(var + LN_EPS) * g + b


def _post_kernel(x_ref, a_ref, wo_ref, win_ref, wout_ref, g1_ref, b1_ref, g2_ref, b2_ref, o_ref):
    h = _dot(a_ref[...], wo_ref[...])
    x1 = _layer_norm(DEEPNORM_ALPHA * x_ref[...] + h, g1_ref[...], b1_ref[...])
    x1b = x1.astype(BF16)
    y = None
    off = 0
    for c in FF_CHUNKS:
        gate = _dot(x1b, win_ref[:, off:off + c])
        up = _dot(x1b, win_ref[:, D_FF + off:D_FF + off + c])
        act = (gate * jax.nn.sigmoid(gate) * up).astype(BF16)
        part = _dot(act, wout_ref[off:off + c, :])
        y = part if y is None else y + part
        off += c
    o_ref[...] = _layer_norm(DEEPNORM_ALPHA * x1 + y, g2_ref[...], b2_ref[...])


def _post(x, a, wo, win, wout, g1, b1, g2, b2):
    S = x.shape[0]
    tm = ROW_TILE
    row = pl.BlockSpec((tm, D_MODEL), lambda i: (i, 0))
    full = lambda arr: _resident(arr.shape, lambda i: (0, 0))
    return pl.pallas_call(
        _post_kernel,
        grid=(S // tm,),
        in_specs=[row, row, full(wo), full(win), full(wout), full(g1), full(b1), full(g2), full(b2)],
        out_specs=row,
        out_shape=jax.ShapeDtypeStruct((S, D_MODEL), F32),
        compiler_params=_params1(),
        name="post",
    )(x, a, wo, win, wout, g1, b1, g2, b2)


def _moba_proj_kernel(x_ref, w_ref, c_ref, sa_ref, sb_ref, q_ref, k_ref, v_ref, kmean_ref):
    H = MOBA_HEADS
    tm = x_ref.shape[0]
    t = pl.program_id(0)
    qscale = (MOBA_HEAD_DIM ** -0.5) * LOG2E
    half = MOBA_ROT_DIM // 2

    @pl.when(t == 0)
    def _():
        kmean_ref[...] = jnp.zeros_like(kmean_ref)

    qkv = _dot(x_ref[...].astype(BF16), w_ref[...])
    cc = c_ref[...]
    sa = sa_ref[...]
    sb = sb_ref[...]

    def rope(z):
        return z * cc + pltpu.roll(z, LANES - half, 1) * sa + pltpu.roll(z, half, 1) * sb

    blocks = tm // MOBA_BLOCK
    nblk = lax.broadcasted_iota(jnp.int32, (LANES, tm), 0).astype(F32)
    qpos = t * tm + lax.broadcasted_iota(jnp.int32, (LANES, tm), 1)
    past = nblk < (qpos // MOBA_BLOCK).astype(F32)
    kmean_row = lax.broadcasted_iota(jnp.int32, (LANES, LANES), 0)
    key_blk = (t * tm + lax.broadcasted_iota(jnp.int32, (tm, LANES), 0)) // MOBA_BLOCK
    blk_onehot = (key_blk == lax.broadcasted_iota(jnp.int32, (tm, LANES), 1)).astype(BF16)
    for h in range(H):
        c = LANES * h
        qh = rope(qkv[:, c:c + LANES])
        kh = rope(qkv[:, H * LANES + c:H * LANES + c + LANES])
        k_ref[:, 2 * c:2 * c + LANES] = kh.astype(BF16)
        k_ref[:, 2 * c + LANES:2 * c + 2 * LANES] = blk_onehot
        v_ref[:, c:c + LANES] = qkv[:, 2 * H * LANES + c:2 * H * LANES + c + LANES].astype(BF16)
        km = kmean_ref[:, c:c + LANES]
        for b in range(blocks):
            mean_b = jnp.mean(kh[b * MOBA_BLOCK:(b + 1) * MOBA_BLOCK], axis=0, keepdims=True)
            km = jnp.where(kmean_row == t * blocks + b, mean_b, km)
        kmean_ref[:, c:c + LANES] = km
        g = _dot_nt(km, qh, precision=lax.Precision.HIGHEST)
        g = jnp.where(past, g, NEG_INF)
        sel = jnp.zeros(g.shape, jnp.bool_)
        for _ in range(MOBA_TOPK):
            best = jnp.max(g, axis=0, keepdims=True)
            idx = jnp.min(jnp.where(g == best, nblk, float(LANES)), axis=0, keepdims=True)
            hit = nblk == idx
            sel = sel | (hit & past)
            g = jnp.where(hit, -jnp.inf, g)
        mask = jnp.where(sel, 0.0, NEG_INF).T
        q_ref[:, 2 * c:2 * c + LANES] = (qh * qscale).astype(BF16)
        q_ref[:, 2 * c + LANES:2 * c + 2 * LANES] = mask.astype(BF16)


def _moba_proj(x, w, cc, sa, sb):
    S = x.shape[0]
    tm = ROW_TILE
    H = MOBA_HEADS
    row = lambda wd: pl.BlockSpec((tm, wd), lambda i: (i, 0))
    return pl.pallas_call(
        _moba_proj_kernel,
        grid=(S // tm,),
        in_specs=[row(D_MODEL), _resident(w.shape, lambda i: (0, 0)), row(LANES), row(LANES), row(LANES)],
        out_specs=[row(2 * LANES * H), row(2 * LANES * H), row(LANES * H)],
        out_shape=[jax.ShapeDtypeStruct((S, 2 * LANES * H), BF16),
                   jax.ShapeDtypeStruct((S, 2 * LANES * H), BF16),
                   jax.ShapeDtypeStruct((S, LANES * H), BF16)],
        scratch_shapes=[pltpu.VMEM((LANES, LANES * H), F32)],
        compiler_params=_params1(),
        name="moba_proj",
    )(x, w, cc, sa, sb)


def _rotary_tables(seq_len, rot_dim):
    inv_freq = ROPE_THETA ** (-jnp.arange(0, rot_dim, 2, dtype=F32) / rot_dim)
    ang = jnp.arange(seq_len, dtype=F32)[:, None] * inv_freq[None, :]
    return jnp.cos(ang), jnp.sin(ang)


def _swap_halves(w, width):
    shp = w.shape
    w = w.reshape(shp[0], -1, 2, width // 2)
    return w[:, :, ::-1, :].reshape(shp)


def kernel(x, mla_w_dqkv, mla_q_norm, mla_w_uq, mla_kv_norm, mla_w_ukv, mla_w_o, moba_w_qkv, moba_w_o,
           ffn_w_in, ffn_w_out, ln_mix_g, ln_mix_b, ln_ffn_g, ln_ffn_b):
    B, S, D = x.shape
    assert B == 1 and D == D_MODEL and S % MOBA_KEY_CHUNK == 0 and S // MOBA_BLOCK <= LANES // 2
    xs = x[0]
    H = MLA_HEADS

    cos, sin = _rotary_tables(S, MLA_ROPE_DIM)
    tc = jnp.concatenate([cos] * 4, axis=1)
    ts = jnp.concatenate([-sin, sin] * 2, axis=1)
    cos, sin = _rotary_tables(S, MOBA_ROT_DIM)
    hr = MOBA_ROT_DIM // 2
    rest = LANES - MOBA_ROT_DIM
    mc = jnp.concatenate([cos, cos, jnp.ones((S, rest), F32)], axis=1)
    msa = jnp.concatenate([-sin, jnp.zeros((S, LANES - hr), F32)], axis=1)
    msb = jnp.concatenate([jnp.zeros((S, hr), F32), sin, jnp.zeros((S, rest), F32)], axis=1)

    wd = mla_w_dqkv[0]
    w_kr = wd[:, MLA_Q_RANK + MLA_KV_RANK:]
    w_kr_sw = _swap_halves(w_kr, MLA_ROPE_DIM)
    wd_ext = jnp.concatenate([wd[:, :MLA_Q_RANK + MLA_KV_RANK], w_kr, w_kr, w_kr_sw, w_kr_sw], axis=1).astype(BF16)
    wq = mla_w_uq[0].reshape(MLA_Q_RANK, H, MLA_QK_DIM)
    wq_nope = wq[:, :, :MLA_NOPE_DIM].reshape(MLA_Q_RANK, H * MLA_NOPE_DIM)
    wq_rope = wq[:, :, MLA_NOPE_DIM:].reshape(MLA_Q_RANK, H * MLA_ROPE_DIM)
    wq_ext = jnp.concatenate([wq_nope, wq_rope, _swap_halves(wq_rope, MLA_ROPE_DIM)], axis=1).astype(BF16)

    q, k, v = _mla_proj(xs, wd_ext, mla_q_norm[0][None, :], wq_ext, mla_kv_norm[0][None, :],
                        mla_w_ukv[0].astype(BF16), tc, ts)
    a = _attention(q, k, v, heads=H, tq=MLA_ATTN_TILE, tk=MLA_ATTN_TILE, moba=False)
    x1 = _post(xs, a, mla_w_o[0].astype(BF16), ffn_w_in[0].astype(BF16), ffn_w_out[0].astype(BF16),
               ln_mix_g[0][None, :], ln_mix_b[0][None, :], ln_ffn_g[0][None, :], ln_ffn_b[0][None, :])

    q, k, v = _moba_proj(x1, moba_w_qkv[0].astype(BF16), mc, msa, msb)
    a = _attention(q, k, v, heads=MOBA_HEADS, tq=MOBA_BLOCK, tk=MOBA_KEY_CHUNK, moba=True)
    x2 = _post(x1, a, moba_w_o[0].astype(BF16), ffn_w_in[1].astype(BF16), ffn_w_out[1].astype(BF16),
               ln_mix_g[1][None, :], ln_mix_b[1][None, :], ln_ffn_g[1][None, :], ln_ffn_b[1][None, :])
    return x2[None]
```

```python
import math

import jax
import jax.numpy as jnp
from jax import lax
from jax.experimental import pallas as pl
from jax.experimental.pallas import tpu as pltpu

D_MODEL = 1024
DEPTH = 2
ROPE_THETA = 500000.0
NEG_INF = -1e30
LN_EPS = 1e-5
RMS_EPS = 1e-6
MLA_HEADS = 8
MLA_Q_RANK = 384
MLA_KV_RANK = 256
MLA_NOPE_DIM = 128
MLA_ROPE_DIM = 64
MLA_V_DIM = 128
MLA_QK_DIM = MLA_NOPE_DIM + MLA_ROPE_DIM
MOBA_HEADS = 8
MOBA_HEAD_DIM = D_MODEL // MOBA_HEADS
MOBA_ROT_DIM = MOBA_HEAD_DIM // 4
MOBA_BLOCK = 256
MOBA_TOPK = 3
D_FF = 2816
DEEPNORM_ALPHA = (2 * DEPTH) ** 0.25

LANES = 128
BF16_SUBLANES = 16
LOG2E = math.log2(math.e)

HEADS = 8
QK_WIDTH = 2 * LANES
V_ROWS = LANES + BF16_SUBLANES
ROW_TILE = 256
ATTN_Q_TILE = 256
ATTN_KEY_CHUNK = 1024
FF_CHUNKS = (768, 768, 768, 512)
VMEM_LIMIT = 48 * 1024 * 1024

BF16 = jnp.bfloat16
F32 = jnp.float32


def _dot(a, b, precision=None):
    return jnp.dot(a, b, preferred_element_type=F32, precision=precision)


def _dot_nt(a, b, precision=None):
    return lax.dot_general(a, b, (((1,), (1,)), ((), ())), preferred_element_type=F32, precision=precision)


def _resident(shape, index_map):
    return pl.BlockSpec(shape, index_map, pipeline_mode=pl.Buffered(1))


def _params(n_grid_dims):
    return pltpu.CompilerParams(dimension_semantics=("arbitrary",) * n_grid_dims, vmem_limit_bytes=VMEM_LIMIT)


def _qkv_out_specs(tm):
    return [pl.BlockSpec((HEADS * QK_WIDTH, tm), lambda i: (0, i)),
            pl.BlockSpec((tm, HEADS * QK_WIDTH), lambda i: (i, 0)),
            pl.BlockSpec((HEADS * V_ROWS, tm), lambda i: (0, i))]


def _qkv_out_shapes(S):
    return [jax.ShapeDtypeStruct((HEADS * QK_WIDTH, S), BF16),
            jax.ShapeDtypeStruct((S, HEADS * QK_WIDTH), BF16),
            jax.ShapeDtypeStruct((HEADS * V_ROWS, S), BF16)]


def _store_vt(vt_ref, h, v):
    tm = v.shape[0]
    r = V_ROWS * h
    vt_ref[r:r + LANES, :] = v.T.astype(BF16)
    pad_row = lax.broadcasted_iota(jnp.int32, (BF16_SUBLANES, tm), 0)
    vt_ref[r + LANES:r + V_ROWS, :] = (pad_row == 0).astype(BF16)


def _rms(x, g):
    return x * lax.rsqrt(jnp.mean(x * x, axis=-1, keepdims=True) + RMS_EPS) * g


def _mla_proj_kernel(x_ref, wd_ref, qn_ref, wq_ref, kvn_ref, wkv_ref, tc_ref, ts_ref,
                     qt_ref, k_ref, vt_ref):
    H = MLA_HEADS
    qscale = (MLA_QK_DIM ** -0.5) * LOG2E
    xb = x_ref[...].astype(BF16)
    lat = _dot(xb, wd_ref[...])
    o1 = MLA_Q_RANK
    o2 = o1 + MLA_KV_RANK
    cq = _rms(lat[:, :o1], qn_ref[...]).astype(BF16)
    ckv = _rms(lat[:, o1:o2], kvn_ref[...]).astype(BF16)
    tc = tc_ref[...]
    ts = ts_ref[...]
    kr = lat[:, o2:o2 + LANES] * tc + lat[:, o2 + LANES:o2 + 2 * LANES] * ts
    lane = lax.broadcasted_iota(jnp.int32, kr.shape, 1)
    kr_lo = jnp.where(lane < MLA_ROPE_DIM, kr, 0.0).astype(BF16)
    kr_hi = jnp.where(lane >= MLA_ROPE_DIM, kr, 0.0).astype(BF16)
    q = _dot(cq, wq_ref[...])
    kv = _dot(ckv, wkv_ref[...])
    r0 = H * MLA_NOPE_DIM
    r1 = r0 + H * MLA_ROPE_DIM
    for p in range(H // 2):
        qr = (q[:, r0 + LANES * p:r0 + LANES * (p + 1)] * tc
              + q[:, r1 + LANES * p:r1 + LANES * (p + 1)] * ts)
        qr_t = (qr * qscale).T.astype(BF16)
        for h in (2 * p, 2 * p + 1):
            c = QK_WIDTH * h
            qt_ref[c:c + LANES, :] = (q[:, LANES * h:LANES * (h + 1)] * qscale).T.astype(BF16)
            qt_ref[c + LANES:c + QK_WIDTH, :] = qr_t
            k_ref[:, c:c + LANES] = kv[:, c:c + LANES].astype(BF16)
            k_ref[:, c + LANES:c + QK_WIDTH] = kr_lo if h % 2 == 0 else kr_hi
            _store_vt(vt_ref, h, kv[:, c + LANES:c + QK_WIDTH])


def _mla_proj(x, wd, qn, wq, kvn, wkv, tc, ts):
    S = x.shape[0]
    tm = ROW_TILE
    row = lambda w: pl.BlockSpec((tm, w), lambda i: (i, 0))
    full = lambda a: _resident(a.shape, lambda i: (0, 0))
    return pl.pallas_call(
        _mla_proj_kernel,
        grid=(S // tm,),
        in_specs=[row(D_MODEL), full(wd), full(qn), full(wq), full(kvn), full(wkv), row(LANES), row(LANES)],
        out_specs=_qkv_out_specs(tm),
        out_shape=_qkv_out_shapes(S),
        compiler_params=_params(1),
        name="mla_proj",
    )(x, wd, qn, wq, kvn, wkv, tc, ts)


def _attn_kernel(qt_ref, k_ref, vt_ref, o_ref, s0_ref, s1_ref, p0_ref, p1_ref, alpha0_ref, alpha1_ref, m_ref, acc_ref):
    tq = qt_ref.shape[1]
    tk = ATTN_KEY_CHUNK
    i = pl.program_id(1)
    n = (i * tq) // tk
    qt = qt_ref[...]
    s_refs = (s0_ref, s1_ref)
    p_refs = (p0_ref, p1_ref)
    alpha_refs = (alpha0_ref, alpha1_ref)

    def chunk_start(c):
        return pl.multiple_of(c * tk, tk)

    def stage_a(c, slot):
        s_refs[slot][...] = _dot(k_ref[pl.ds(chunk_start(c), tk), :], qt)

    def stage_b(slot, causal_chunk=None):
        s = s_refs[slot][...]
        if causal_chunk is not None:
            kpos = causal_chunk * tk + lax.broadcasted_iota(jnp.int32, s.shape, 0)
            qpos = i * tq + lax.broadcasted_iota(jnp.int32, s.shape, 1)
            s = jnp.where(kpos <= qpos, s, NEG_INF)
        m_prev = m_ref[...]
        m_new = jnp.maximum(m_prev, jnp.max(s, axis=0, keepdims=True))
        alpha_refs[slot][...] = jnp.exp2(m_prev - m_new)
        p_refs[slot][...] = jnp.exp2(s - m_new).astype(BF16)
        m_ref[...] = m_new

    def stage_c(c, slot):
        pv = _dot(vt_ref[:, pl.ds(chunk_start(c), tk)], p_refs[slot][...])
        acc_ref[...] = alpha_refs[slot][...] * acc_ref[...] + pv

    def step(t, slot):
        stage_c(jnp.maximum(t - 1, 0), 1 - slot)
        stage_b(slot)
        stage_a(t + 1, 1 - slot)

    def finish(slot):
        stage_c(jnp.maximum(n - 1, 0), 1 - slot)
        stage_b(slot, causal_chunk=n)
        stage_c(n, slot)
        acc = acc_ref[...]
        o_ref[...] = (acc[:LANES] / acc[LANES:LANES + 1]).T.astype(o_ref.dtype)

    m_ref[...] = jnp.full(m_ref.shape, -jnp.inf, F32)
    acc_ref[...] = jnp.zeros_like(acc_ref)
    p1_ref[...] = jnp.zeros_like(p1_ref)
    alpha1_ref[...] = jnp.ones_like(alpha1_ref)
    stage_a(0, 0)

    def pair(u, carry):
        step(2 * u, 0)
        step(2 * u + 1, 1)
        return carry

    lax.fori_loop(0, n // 2, pair, 0)

    @pl.when(n % 2 == 1)
    def _():
        step(n - 1, 0)
        finish(1)

    @pl.when(n % 2 == 0)
    def _():
        finish(0)


def _attention(qt, k, vt, *, name):
    S = k.shape[0]
    tq = ATTN_Q_TILE
    return pl.pallas_call(
        _attn_kernel,
        grid=(HEADS, S // tq),
        in_specs=[pl.BlockSpec((QK_WIDTH, tq), lambda h, i: (h, i)),
                  _resident((S, QK_WIDTH), lambda h, i: (0, h)),
                  _resident((V_ROWS, S), lambda h, i: (h, 0))],
        out_specs=pl.BlockSpec((tq, LANES), lambda h, i: (i, h)),
        out_shape=jax.ShapeDtypeStruct((S, LANES * HEADS), BF16),
        scratch_shapes=[pltpu.VMEM((ATTN_KEY_CHUNK, tq), F32)] * 2
        + [pltpu.VMEM((ATTN_KEY_CHUNK, tq), BF16)] * 2
        + [pltpu.VMEM((1, tq), F32)] * 2
        + [pltpu.VMEM((1, tq), F32),
           pltpu.VMEM((V_ROWS, tq), F32)],
        compiler_params=_params(2),
        name=name,
    )(qt, k, vt)


def _layer_norm(z, g, b):
    mu = jnp.mean(z, axis=-1, keepdims=True)
    d = z - mu
    var = jnp.mean(d * d, axis=-1, keepdims=True)
    return d * lax.rsqrt(var + LN_EPS) * g + b


def _post_kernel(x_ref, a_ref, wo_ref, win_ref, wout_ref, g1_ref, b1_ref, g2_ref, b2_ref, o_ref):
    h = _dot(a_ref[...], wo_ref[...])
    x1 = _layer_norm(DEEPNORM_ALPHA * x_ref[...] + h, g1_ref[...], b1_ref[...])
    x1b = x1.astype(BF16)
    y = None
    off = 0
    for c in FF_CHUNKS:
        gate = _dot(x1b, win_ref[:, off:off + c])
        up = _dot(x1b, win_ref[:, D_FF + off:D_FF + off + c])
        act = (gate * jax.nn.sigmoid(gate) * up).astype(BF16)
        part = _dot(act, wout_ref[off:off + c, :])
        y = part if y is None else y + part
        off += c
    o_ref[...] = _layer_norm(DEEPNORM_ALPHA * x1 + y, g2_ref[...], b2_ref[...])


def _post(x, a, wo, win, wout, g1, b1, g2, b2):
    S = x.shape[0]
    tm = ROW_TILE
    row = pl.BlockSpec((tm, D_MODEL), lambda i: (i, 0))
    full = lambda arr: _resident(arr.shape, lambda i: (0, 0))
    return pl.pallas_call(
        _post_kernel,
        grid=(S // tm,),
        in_specs=[row, row, full(wo), full(win), full(wout), full(g1), full(b1), full(g2), full(b2)],
        out_specs=row,
        out_shape=jax.ShapeDtypeStruct((S, D_MODEL), F32),
        compiler_params=_params(1),
        name="post",
    )(x, a, wo, win, wout, g1, b1, g2, b2)


def _moba_proj_kernel(x_ref, w_ref, c_ref, sa_ref, sb_ref, qt_ref, k_ref, vt_ref, kmean_ref):
    H = MOBA_HEADS
    tm = x_ref.shape[0]
    t = pl.program_id(0)
    qscale = (MOBA_HEAD_DIM ** -0.5) * LOG2E
    half = MOBA_ROT_DIM // 2

    @pl.when(t == 0)
    def _():
        kmean_ref[...] = jnp.zeros_like(kmean_ref)

    qkv = _dot(x_ref[...].astype(BF16), w_ref[...])
    cc = c_ref[...]
    sa = sa_ref[...]
    sb = sb_ref[...]

    def rope(z):
        return z * cc + pltpu.roll(z, LANES - half, 1) * sa + pltpu.roll(z, half, 1) * sb

    blocks = tm // MOBA_BLOCK
    nblk = lax.broadcasted_iota(jnp.int32, (LANES, tm), 0).astype(F32)
    qpos = t * tm + lax.broadcasted_iota(jnp.int32, (LANES, tm), 1)
    own = (qpos // MOBA_BLOCK).astype(F32)
    past = nblk < own
    kmean_row = lax.broadcasted_iota(jnp.int32, (LANES, LANES), 0)
    key_blk = (t * tm + lax.broadcasted_iota(jnp.int32, (tm, LANES), 0)) // MOBA_BLOCK
    blk_onehot = (key_blk == lax.broadcasted_iota(jnp.int32, (tm, LANES), 1)).astype(BF16)
    for h in range(H):
        c = LANES * h
        qh = rope(qkv[:, c:c + LANES])
        kh = rope(qkv[:, H * LANES + c:H * LANES + c + LANES])
        k_ref[:, 2 * c:2 * c + LANES] = kh.astype(BF16)
        k_ref[:, 2 * c + LANES:2 * c + QK_WIDTH] = blk_onehot
        _store_vt(vt_ref, h, qkv[:, 2 * H * LANES + c:2 * H * LANES + c + LANES])
        km = kmean_ref[:, c:c + LANES]
        for b in range(blocks):
            mean_b = jnp.mean(kh[b * MOBA_BLOCK:(b + 1) * MOBA_BLOCK], axis=0, keepdims=True)
            km = jnp.where(kmean_row == t * blocks + b, mean_b, km)
        kmean_ref[:, c:c + LANES] = km
        g = _dot_nt(km, qh, precision=lax.Precision.HIGHEST)
        g = jnp.where(past, g, NEG_INF)
        sel = nblk == own
        for _ in range(MOBA_TOPK):
            best = jnp.max(g, axis=0, keepdims=True)
            idx = jnp.min(jnp.where(g == best, nblk, float(LANES)), axis=0, keepdims=True)
            hit = nblk == idx
            sel = sel | (hit & past)
            g = jnp.where(hit, -jnp.inf, g)
        qt_ref[2 * c:2 * c + LANES, :] = (qh * qscale).T.astype(BF16)
        qt_ref[2 * c + LANES:2 * c + QK_WIDTH, :] = jnp.where(sel, 0.0, NEG_INF).astype(BF16)


def _moba_proj(x, w, cc, sa, sb):
    S = x.shape[0]
    tm = ROW_TILE
    row = lambda wd: pl.BlockSpec((tm, wd), lambda i: (i, 0))
    return pl.pallas_call(
        _moba_proj_kernel,
        grid=(S // tm,),
        in_specs=[row(D_MODEL), _resident(w.shape, lambda i: (0, 0)), row(LANES), row(LANES), row(LANES)],
        out_specs=_qkv_out_specs(tm),
        out_shape=_qkv_out_shapes(S),
        scratch_shapes=[pltpu.VMEM((LANES, LANES * MOBA_HEADS), F32)],
        compiler_params=_params(1),
        name="moba_proj",
    )(x, w, cc, sa, sb)


def _rotary_tables(seq_len, rot_dim):
    inv_freq = ROPE_THETA ** (-jnp.arange(0, rot_dim, 2, dtype=F32) / rot_dim)
    ang = jnp.arange(seq_len, dtype=F32)[:, None] * inv_freq[None, :]
    return jnp.cos(ang), jnp.sin(ang)


def _swap_halves(w, width):
    shp = w.shape
    w = w.reshape(shp[0], -1, 2, width // 2)
    return w[:, :, ::-1, :].reshape(shp)


def kernel(x, mla_w_dqkv, mla_q_norm, mla_w_uq, mla_kv_norm, mla_w_ukv, mla_w_o, moba_w_qkv, moba_w_o,
           ffn_w_in, ffn_w_out, ln_mix_g, ln_mix_b, ln_ffn_g, ln_ffn_b):
    B, S, D = x.shape
    assert B == 1 and D == D_MODEL and S % ATTN_KEY_CHUNK == 0 and S // MOBA_BLOCK <= LANES
    assert MLA_HEADS == HEADS and MOBA_HEADS == HEADS and ATTN_Q_TILE == MOBA_BLOCK
    xs = x[0]
    H = MLA_HEADS

    cos, sin = _rotary_tables(S, MLA_ROPE_DIM)
    tc = jnp.concatenate([cos] * 4, axis=1)
    ts = jnp.concatenate([-sin, sin] * 2, axis=1)
    cos, sin = _rotary_tables(S, MOBA_ROT_DIM)
    hr = MOBA_ROT_DIM // 2
    rest = LANES - MOBA_ROT_DIM
    mc = jnp.concatenate([cos, cos, jnp.ones((S, rest), F32)], axis=1)
    msa = jnp.concatenate([-sin, jnp.zeros((S, LANES - hr), F32)], axis=1)
    msb = jnp.concatenate([jnp.zeros((S, hr), F32), sin, jnp.zeros((S, rest), F32)], axis=1)

    wd = mla_w_dqkv[0]
    w_kr = wd[:, MLA_Q_RANK + MLA_KV_RANK:]
    w_kr_sw = _swap_halves(w_kr, MLA_ROPE_DIM)
    wd_ext = jnp.concatenate([wd[:, :MLA_Q_RANK + MLA_KV_RANK], w_kr, w_kr, w_kr_sw, w_kr_sw], axis=1).astype(BF16)
    wq = mla_w_uq[0].reshape(MLA_Q_RANK, H, MLA_QK_DIM)
    wq_nope = wq[:, :, :MLA_NOPE_DIM].reshape(MLA_Q_RANK, H * MLA_NOPE_DIM)
    wq_rope = wq[:, :, MLA_NOPE_DIM:].reshape(MLA_Q_RANK, H * MLA_ROPE_DIM)
    wq_ext = jnp.concatenate([wq_nope, wq_rope, _swap_halves(wq_rope, MLA_ROPE_DIM)], axis=1).astype(BF16)

    qt, k, vt = _mla_proj(xs, wd_ext, mla_q_norm[0][None, :], wq_ext, mla_kv_norm[0][None, :],
                          mla_w_ukv[0].astype(BF16), tc, ts)
    a = _attention(qt, k, vt, name="mla_attn")
    x1 = _post(xs, a, mla_w_o[0].astype(BF16), ffn_w_in[0].astype(BF16), ffn_w_out[0].astype(BF16),
               ln_mix_g[0][None, :], ln_mix_b[0][None, :], ln_ffn_g[0][None, :], ln_ffn_b[0][None, :])

    qt, k, vt = _moba_proj(x1, moba_w_qkv[0].astype(BF16), mc, msa, msb)
    a = _attention(qt, k, vt, name="moba_attn")
    x2 = _post(x1, a, moba_w_o[0].astype(BF16), ffn_w_in[1].astype(BF16), ffn_w_out[1].astype(BF16),
               ln_mix_g[1][None, :], ln_mix_b[1][None, :], ln_ffn_g[1][None, :], ln_ffn_b[1][None, :])
    return x2[None]
```

```python
import math

import jax
import jax.numpy as jnp
from jax import lax
from jax.experimental import pallas as pl
from jax.experimental.pallas import tpu as pltpu

D_MODEL = 1024
DEPTH = 2
ROPE_THETA = 500000.0
NEG_INF = -1e30
LN_EPS = 1e-5
RMS_EPS = 1e-6
MLA_HEADS = 8
MLA_Q_RANK = 384
MLA_KV_RANK = 256
MLA_NOPE_DIM = 128
MLA_ROPE_DIM = 64
MLA_V_DIM = 128
MLA_QK_DIM = MLA_NOPE_DIM + MLA_ROPE_DIM
MOBA_HEADS = 8
MOBA_HEAD_DIM = D_MODEL // MOBA_HEADS
MOBA_ROT_DIM = MOBA_HEAD_DIM // 4
MOBA_BLOCK = 256
MOBA_TOPK = 3
D_FF = 2816
DEEPNORM_ALPHA = (2 * DEPTH) ** 0.25

LANES = 128
BF16_SUBLANES = 16
LOG2E = math.log2(math.e)

HEADS = 8
QK_WIDTH = 2 * LANES
V_ROWS = LANES + BF16_SUBLANES
ROW_TILE = 256
ATTN_Q_TILE = 512
ATTN_KEY_CHUNK = 1024
FF_CHUNKS = (768, 768, 768, 512)
VMEM_LIMIT = 48 * 1024 * 1024

BF16 = jnp.bfloat16
F32 = jnp.float32


def _dot(a, b, precision=None):
    return jnp.dot(a, b, preferred_element_type=F32, precision=precision)


def _dot_nt(a, b, precision=None):
    return lax.dot_general(a, b, (((1,), (1,)), ((), ())), preferred_element_type=F32, precision=precision)


def _resident(shape, index_map):
    return pl.BlockSpec(shape, index_map, pipeline_mode=pl.Buffered(1))


def _params(n_grid_dims):
    return pltpu.CompilerParams(dimension_semantics=("arbitrary",) * n_grid_dims, vmem_limit_bytes=VMEM_LIMIT)


def _qkv_out_specs(tm):
    return [pl.BlockSpec((HEADS * QK_WIDTH, tm), lambda i: (0, i)),
            pl.BlockSpec((tm, HEADS * QK_WIDTH), lambda i: (i, 0)),
            pl.BlockSpec((HEADS * V_ROWS, tm), lambda i: (0, i))]


def _qkv_out_shapes(S):
    return [jax.ShapeDtypeStruct((HEADS * QK_WIDTH, S), BF16),
            jax.ShapeDtypeStruct((S, HEADS * QK_WIDTH), BF16),
            jax.ShapeDtypeStruct((HEADS * V_ROWS, S), BF16)]


def _store_vt(vt_ref, h, v):
    tm = v.shape[0]
    r = V_ROWS * h
    vt_ref[r:r + LANES, :] = v.T.astype(BF16)
    pad_row = lax.broadcasted_iota(jnp.int32, (BF16_SUBLANES, tm), 0)
    vt_ref[r + LANES:r + V_ROWS, :] = (pad_row == 0).astype(BF16)


def _rms(x, g):
    return x * lax.rsqrt(jnp.mean(x * x, axis=-1, keepdims=True) + RMS_EPS) * g


def _mla_proj_kernel(x_ref, wd_ref, qn_ref, wq_ref, kvn_ref, wkv_ref, tc_ref, ts_ref,
                     qt_ref, k_ref, vt_ref):
    H = MLA_HEADS
    qscale = (MLA_QK_DIM ** -0.5) * LOG2E
    xb = x_ref[...].astype(BF16)
    lat = _dot(xb, wd_ref[...])
    o1 = MLA_Q_RANK
    o2 = o1 + MLA_KV_RANK
    cq = _rms(lat[:, :o1], qn_ref[...]).astype(BF16)
    ckv = _rms(lat[:, o1:o2], kvn_ref[...]).astype(BF16)
    tc = tc_ref[...]
    ts = ts_ref[...]
    kr = lat[:, o2:o2 + LANES] * tc + lat[:, o2 + LANES:o2 + 2 * LANES] * ts
    lane = lax.broadcasted_iota(jnp.int32, kr.shape, 1)
    kr_lo = jnp.where(lane < MLA_ROPE_DIM, kr, 0.0).astype(BF16)
    kr_hi = jnp.where(lane >= MLA_ROPE_DIM, kr, 0.0).astype(BF16)
    q = _dot(cq, wq_ref[...])
    kv = _dot(ckv, wkv_ref[...])
    r0 = H * MLA_NOPE_DIM
    r1 = r0 + H * MLA_ROPE_DIM
    for p in range(H // 2):
        qr = (q[:, r0 + LANES * p:r0 + LANES * (p + 1)] * tc
              + q[:, r1 + LANES * p:r1 + LANES * (p + 1)] * ts)
        qr_t = (qr * qscale).T.astype(BF16)
        for h in (2 * p, 2 * p + 1):
            c = QK_WIDTH * h
            qt_ref[c:c + LANES, :] = (q[:, LANES * h:LANES * (h + 1)] * qscale).T.astype(BF16)
            qt_ref[c + LANES:c + QK_WIDTH, :] = qr_t
            k_ref[:, c:c + LANES] = kv[:, c:c + LANES].astype(BF16)
            k_ref[:, c + LANES:c + QK_WIDTH] = kr_lo if h % 2 == 0 else kr_hi
            _store_vt(vt_ref, h, kv[:, c + LANES:c + QK_WIDTH])


def _mla_proj(x, wd, qn, wq, kvn, wkv, tc, ts):
    S = x.shape[0]
    tm = ROW_TILE
    row = lambda w: pl.BlockSpec((tm, w), lambda i: (i, 0))
    full = lambda a: _resident(a.shape, lambda i: (0, 0))
    return pl.pallas_call(
        _mla_proj_kernel,
        grid=(S // tm,),
        in_specs=[row(D_MODEL), full(wd), full(qn), full(wq), full(kvn), full(wkv), row(LANES), row(LANES)],
        out_specs=_qkv_out_specs(tm),
        out_shape=_qkv_out_shapes(S),
        compiler_params=_params(1),
        name="mla_proj",
    )(x, wd, qn, wq, kvn, wkv, tc, ts)


EXP_ROWS = 16
PIPE_SLOTS = 4


def _attn_kernel(qt_ref, k_ref, vt_ref, o_ref, *scratch):
    ns = PIPE_SLOTS
    s_refs, cmax_refs, p_refs, alpha_refs = (scratch[j * ns:(j + 1) * ns] for j in range(4))
    m_ref, acc_ref = scratch[4 * ns:]
    tq = qt_ref.shape[1]
    tk = ATTN_KEY_CHUNK
    i = pl.program_id(1)
    n = (i * tq) // tk

    def chunk_start(c):
        return pl.multiple_of(jnp.clip(c, 0, n) * tk, tk)

    def stage_a(c, slot):
        s = _dot(k_ref[pl.ds(chunk_start(c), tk), :], qt_ref[...])
        s_refs[slot][...] = s
        cmax_refs[slot][...] = jnp.max(s, axis=0, keepdims=True)

    def stage_b(slot, causal_chunk=None):
        s_ref = s_refs[slot]
        if causal_chunk is None:
            cmax = cmax_refs[slot][...]
        else:
            kpos = causal_chunk * tk + lax.broadcasted_iota(jnp.int32, s_ref.shape, 0)
            qpos = i * tq + lax.broadcasted_iota(jnp.int32, s_ref.shape, 1)
            s = jnp.where(kpos <= qpos, s_ref[...], NEG_INF)
            s_ref[...] = s
            cmax = jnp.max(s, axis=0, keepdims=True)
        m_prev = m_ref[...]
        m_new = jnp.maximum(m_prev, cmax)
        alpha_refs[slot][...] = jnp.exp2(m_prev - m_new)
        m_ref[...] = m_new
        for r in range(0, tk, EXP_ROWS):
            p_refs[slot][r:r + EXP_ROWS, :] = jnp.exp2(s_ref[r:r + EXP_ROWS, :] - m_new).astype(BF16)

    def stage_c(c, slot):
        pv = _dot(vt_ref[:, pl.ds(chunk_start(c), tk)], p_refs[slot][...])
        acc_ref[...] = alpha_refs[slot][...] * acc_ref[...] + pv

    def group(u, x, y, with_c=True):
        if with_c:
            stage_c(2 * u - 2, y[0])
            stage_c(2 * u - 1, y[1])
        stage_b(x[0])
        stage_b(x[1])
        stage_a(2 * u + 2, y[0])
        stage_a(2 * u + 3, y[1])

    def finish(x, y, n_is_odd, with_c=True):
        if with_c:
            stage_c(n - n % 2 - 2, y[0])
            stage_c(n - n % 2 - 1, y[1])
        if n_is_odd:
            stage_b(x[0])
            stage_b(x[1], causal_chunk=n)
            stage_c(n - 1, x[0])
            stage_c(n, x[1])
        else:
            stage_b(x[0], causal_chunk=n)
            stage_c(n, x[0])
        acc = acc_ref[...]
        o_ref[...] = (acc[:LANES] / acc[LANES:LANES + 1]).T.astype(o_ref.dtype)

    lo, hi = (0, 1), (2, 3)
    m_ref[...] = jnp.full(m_ref.shape, -jnp.inf, F32)
    acc_ref[...] = jnp.zeros_like(acc_ref)
    stage_a(0, lo[0])
    stage_a(1, lo[1])

    @pl.when(n >= 2)
    def _():
        group(0, lo, hi, with_c=False)

    def body(u, carry):
        @pl.when(u % 2 == 0)
        def _():
            group(u, lo, hi)

        @pl.when(u % 2 == 1)
        def _():
            group(u, hi, lo)

        return carry

    lax.fori_loop(1, n // 2, body, 0)
    for r in range(2):
        @pl.when(n == r)
        def _(r=r):
            finish(lo, hi, n_is_odd=bool(r), with_c=False)

    for r in range(4):
        @pl.when((n >= 2) & (n % 4 == r))
        def _(r=r):
            x, y = (lo, hi) if r < 2 else (hi, lo)
            finish(x, y, n_is_odd=bool(r % 2))


def _attention(qt, k, vt, *, name):
    S = k.shape[0]
    tq = ATTN_Q_TILE
    tk = ATTN_KEY_CHUNK
    return pl.pallas_call(
        _attn_kernel,
        grid=(HEADS, S // tq),
        in_specs=[pl.BlockSpec((QK_WIDTH, tq), lambda h, i: (h, i)),
                  _resident((S, QK_WIDTH), lambda h, i: (0, h)),
                  _resident((V_ROWS, S), lambda h, i: (h, 0))],
        out_specs=pl.BlockSpec((tq, LANES), lambda h, i: (i, h)),
        out_shape=jax.ShapeDtypeStruct((S, LANES * HEADS), BF16),
        scratch_shapes=[pltpu.VMEM((tk, tq), F32)] * PIPE_SLOTS
        + [pltpu.VMEM((1, tq), F32)] * PIPE_SLOTS
        + [pltpu.VMEM((tk, tq), BF16)] * PIPE_SLOTS
        + [pltpu.VMEM((1, tq), F32)] * PIPE_SLOTS
        + [pltpu.VMEM((1, tq), F32),
           pltpu.VMEM((V_ROWS, tq), F32)],
        compiler_params=_params(2),
        name=name,
    )(qt, k, vt)


def _layer_norm(z, g, b):
    mu = jnp.mean(z, axis=-1, keepdims=True)
    d = z - mu
    var = jnp.mean(d * d, axis=-1, keepdims=True)
    return d * lax.rsqrt(var + LN_EPS) * g + b


def _post_kernel(x_ref, a_ref, wo_ref, win_ref, wout_ref, g1_ref, b1_ref, g2_ref, b2_ref, o_ref):
    h = _dot(a_ref[...], wo_ref[...])
    x1 = _layer_norm(DEEPNORM_ALPHA * x_ref[...] + h, g1_ref[...], b1_ref[...])
    x1b = x1.astype(BF16)
    y = None
    off = 0
    for c in FF_CHUNKS:
        gate = _dot(x1b, win_ref[:, off:off + c])
        up = _dot(x1b, win_ref[:, D_FF + off:D_FF + off + c])
        act = (gate * jax.nn.sigmoid(gate) * up).astype(BF16)
        part = _dot(act, wout_ref[off:off + c, :])
        y = part if y is None else y + part
        off += c
    o_ref[...] = _layer_norm(DEEPNORM_ALPHA * x1 + y, g2_ref[...], b2_ref[...])


def _post(x, a, wo, win, wout, g1, b1, g2, b2):
    S = x.shape[0]
    tm = ROW_TILE
    row = pl.BlockSpec((tm, D_MODEL), lambda i: (i, 0))
    full = lambda arr: _resident(arr.shape, lambda i: (0, 0))
    return pl.pallas_call(
        _post_kernel,
        grid=(S // tm,),
        in_specs=[row, row, full(wo), full(win), full(wout), full(g1), full(b1), full(g2), full(b2)],
        out_specs=row,
        out_shape=jax.ShapeDtypeStruct((S, D_MODEL), F32),
        compiler_params=_params(1),
        name="post",
    )(x, a, wo, win, wout, g1, b1, g2, b2)


def _moba_proj_kernel(x_ref, w_ref, c_ref, sa_ref, sb_ref, qt_ref, k_ref, vt_ref, kmean_ref):
    H = MOBA_HEADS
    tm = x_ref.shape[0]
    t = pl.program_id(0)
    qscale = (MOBA_HEAD_DIM ** -0.5) * LOG2E
    half = MOBA_ROT_DIM // 2

    @pl.when(t == 0)
    def _():
        kmean_ref[...] = jnp.zeros_like(kmean_ref)

    qkv = _dot(x_ref[...].astype(BF16), w_ref[...])
    cc = c_ref[...]
    sa = sa_ref[...]
    sb = sb_ref[...]

    def rope(z):
        return z * cc + pltpu.roll(z, LANES - half, 1) * sa + pltpu.roll(z, half, 1) * sb

    blocks = tm // MOBA_BLOCK
    nblk = lax.broadcasted_iota(jnp.int32, (LANES, tm), 0).astype(F32)
    qpos = t * tm + lax.broadcasted_iota(jnp.int32, (LANES, tm), 1)
    own = (qpos // MOBA_BLOCK).astype(F32)
    past = nblk < own
    kmean_row = lax.broadcasted_iota(jnp.int32, (LANES, LANES), 0)
    key_blk = (t * tm + lax.broadcasted_iota(jnp.int32, (tm, LANES), 0)) // MOBA_BLOCK
    blk_onehot = (key_blk == lax.broadcasted_iota(jnp.int32, (tm, LANES), 1)).astype(BF16)
    for h in range(H):
        c = LANES * h
        qh = rope(qkv[:, c:c + LANES])
        kh = rope(qkv[:, H * LANES + c:H * LANES + c + LANES])
        k_ref[:, 2 * c:2 * c + LANES] = kh.astype(BF16)
        k_ref[:, 2 * c + LANES:2 * c + QK_WIDTH] = blk_onehot
        _store_vt(vt_ref, h, qkv[:, 2 * H * LANES + c:2 * H * LANES + c + LANES])
        km = kmean_ref[:, c:c + LANES]
        for b in range(blocks):
            mean_b = jnp.mean(kh[b * MOBA_BLOCK:(b + 1) * MOBA_BLOCK], axis=0, keepdims=True)
            km = jnp.where(kmean_row == t * blocks + b, mean_b, km)
        kmean_ref[:, c:c + LANES] = km
        g = _dot_nt(km, qh, precision=lax.Precision.HIGHEST)
        g = jnp.where(past, g, NEG_INF)
        sel = nblk == own
        for _ in range(MOBA_TOPK):
            best = jnp.max(g, axis=0, keepdims=True)
            idx = jnp.min(jnp.where(g == best, nblk, float(LANES)), axis=0, keepdims=True)
            hit = nblk == idx
            sel = sel | (hit & past)
            g = jnp.where(hit, -jnp.inf, g)
        qt_ref[2 * c:2 * c + LANES, :] = (qh * qscale).T.astype(BF16)
        qt_ref[2 * c + LANES:2 * c + QK_WIDTH, :] = jnp.where(sel, 0.0, NEG_INF).astype(BF16)


def _moba_proj(x, w, cc, sa, sb):
    S = x.shape[0]
    tm = ROW_TILE
    row = lambda wd: pl.BlockSpec((tm, wd), lambda i: (i, 0))
    return pl.pallas_call(
        _moba_proj_kernel,
        grid=(S // tm,),
        in_specs=[row(D_MODEL), _resident(w.shape, lambda i: (0, 0)), row(LANES), row(LANES), row(LANES)],
        out_specs=_qkv_out_specs(tm),
        out_shape=_qkv_out_shapes(S),
        scratch_shapes=[pltpu.VMEM((LANES, LANES * MOBA_HEADS), F32)],
        compiler_params=_params(1),
        name="moba_proj",
    )(x, w, cc, sa, sb)


def _rotary_tables(seq_len, rot_dim):
    inv_freq = ROPE_THETA ** (-jnp.arange(0, rot_dim, 2, dtype=F32) / rot_dim)
    ang = jnp.arange(seq_len, dtype=F32)[:, None] * inv_freq[None, :]
    return jnp.cos(ang), jnp.sin(ang)


def _swap_halves(w, width):
    shp = w.shape
    w = w.reshape(shp[0], -1, 2, width // 2)
    return w[:, :, ::-1, :].reshape(shp)


def kernel(x, mla_w_dqkv, mla_q_norm, mla_w_uq, mla_kv_norm, mla_w_ukv, mla_w_o, moba_w_qkv, moba_w_o,
           ffn_w_in, ffn_w_out, ln_mix_g, ln_mix_b, ln_ffn_g, ln_ffn_b):
    B, S, D = x.shape
    assert B == 1 and D == D_MODEL and S % ATTN_KEY_CHUNK == 0 and S // MOBA_BLOCK <= LANES
    assert MLA_HEADS == HEADS and MOBA_HEADS == HEADS and ATTN_KEY_CHUNK % ATTN_Q_TILE == 0
    xs = x[0]
    H = MLA_HEADS

    cos, sin = _rotary_tables(S, MLA_ROPE_DIM)
    tc = jnp.concatenate([cos] * 4, axis=1)
    ts = jnp.concatenate([-sin, sin] * 2, axis=1)
    cos, sin = _rotary_tables(S, MOBA_ROT_DIM)
    hr = MOBA_ROT_DIM // 2
    rest = LANES - MOBA_ROT_DIM
    mc = jnp.concatenate([cos, cos, jnp.ones((S, rest), F32)], axis=1)
    msa = jnp.concatenate([-sin, jnp.zeros((S, LANES - hr), F32)], axis=1)
    msb = jnp.concatenate([jnp.zeros((S, hr), F32), sin, jnp.zeros((S, rest), F32)], axis=1)

    wd = mla_w_dqkv[0]
    w_kr = wd[:, MLA_Q_RANK + MLA_KV_RANK:]
    w_kr_sw = _swap_halves(w_kr, MLA_ROPE_DIM)
    wd_ext = jnp.concatenate([wd[:, :MLA_Q_RANK + MLA_KV_RANK], w_kr, w_kr, w_kr_sw, w_kr_sw], axis=1).astype(BF16)
    wq = mla_w_uq[0].reshape(MLA_Q_RANK, H, MLA_QK_DIM)
    wq_nope = wq[:, :, :MLA_NOPE_DIM].reshape(MLA_Q_RANK, H * MLA_NOPE_DIM)
    wq_rope = wq[:, :, MLA_NOPE_DIM:].reshape(MLA_Q_RANK, H * MLA_ROPE_DIM)
    wq_ext = jnp.concatenate([wq_nope, wq_rope, _swap_halves(wq_rope, MLA_ROPE_DIM)], axis=1).astype(BF16)

    qt, k, vt = _mla_proj(xs, wd_ext, mla_q_norm[0][None, :], wq_ext, mla_kv_norm[0][None, :],
                          mla_w_ukv[0].astype(BF16), tc, ts)
    a = _attention(qt, k, vt, name="mla_attn")
    x1 = _post(xs, a, mla_w_o[0].astype(BF16), ffn_w_in[0].astype(BF16), ffn_w_out[0].astype(BF16),
               ln_mix_g[0][None, :], ln_mix_b[0][None, :], ln_ffn_g[0][None, :], ln_ffn_b[0][None, :])

    qt, k, vt = _moba_proj(x1, moba_w_qkv[0].astype(BF16), mc, msa, msb)
    a = _attention(qt, k, vt, name="moba_attn")
    x2 = _post(x1, a, moba_w_o[0].astype(BF16), ffn_w_in[1].astype(BF16), ffn_w_out[1].astype(BF16),
               ln_mix_g[1][None, :], ln_mix_b[1][None, :], ln_ffn_g[1][None, :], ln_ffn_b[1][None, :])
    return x2[None]
```

```python
import math

import jax
import jax.numpy as jnp
from jax import lax
from jax.experimental import pallas as pl
from jax.experimental.pallas import tpu as pltpu

D_MODEL = 1024
DEPTH = 2
ROPE_THETA = 500000.0
NEG_INF = -1e30
LN_EPS = 1e-5
RMS_EPS = 1e-6
MLA_HEADS = 8
MLA_Q_RANK = 384
MLA_KV_RANK = 256
MLA_NOPE_DIM = 128
MLA_ROPE_DIM = 64
MLA_V_DIM = 128
MLA_QK_DIM = MLA_NOPE_DIM + MLA_ROPE_DIM
MOBA_HEADS = 8
MOBA_HEAD_DIM = D_MODEL // MOBA_HEADS
MOBA_ROT_DIM = MOBA_HEAD_DIM // 4
MOBA_BLOCK = 256
MOBA_TOPK = 3
D_FF = 2816
DEEPNORM_ALPHA = (2 * DEPTH) ** 0.25

LANES = 128
BF16_SUBLANES = 16
LOG2E = math.log2(math.e)

HEADS = 8
QK_WIDTH = 2 * LANES
V_ROWS = LANES + BF16_SUBLANES
PROJ_ROW_TILE = 512
FFN_ROW_TILE = 512
ATTN_Q_TILE = 512
ATTN_KEY_CHUNK = 1024
FF_CHUNKS = (768, 768, 768, 512)
VMEM_LIMIT = 48 * 1024 * 1024

BF16 = jnp.bfloat16
F32 = jnp.float32


def _dot(a, b, precision=None):
    return jnp.dot(a, b, preferred_element_type=F32, precision=precision)


def _dot_nt(a, b, precision=None):
    return lax.dot_general(a, b, (((1,), (1,)), ((), ())), preferred_element_type=F32, precision=precision)


def _resident(shape, index_map):
    return pl.BlockSpec(shape, index_map, pipeline_mode=pl.Buffered(1))


def _params(n_grid_dims):
    return pltpu.CompilerParams(dimension_semantics=("arbitrary",) * n_grid_dims, vmem_limit_bytes=VMEM_LIMIT)


def _qkv_out_specs(tm):
    return [pl.BlockSpec((HEADS * QK_WIDTH, tm), lambda i: (0, i)),
            pl.BlockSpec((tm, HEADS * QK_WIDTH), lambda i: (i, 0)),
            pl.BlockSpec((HEADS * V_ROWS, tm), lambda i: (0, i))]


def _qkv_out_shapes(S):
    return [jax.ShapeDtypeStruct((HEADS * QK_WIDTH, S), BF16),
            jax.ShapeDtypeStruct((S, HEADS * QK_WIDTH), BF16),
            jax.ShapeDtypeStruct((HEADS * V_ROWS, S), BF16)]


def _store_vt(vt_ref, h, v):
    tm = v.shape[0]
    r = V_ROWS * h
    vt_ref[r:r + LANES, :] = v.T.astype(BF16)
    pad_row = lax.broadcasted_iota(jnp.int32, (BF16_SUBLANES, tm), 0)
    vt_ref[r + LANES:r + V_ROWS, :] = (pad_row == 0).astype(BF16)


def _rms(x, g):
    return x * lax.rsqrt(jnp.mean(x * x, axis=-1, keepdims=True) + RMS_EPS) * g


def _mla_proj_kernel(x_ref, wd_ref, qn_ref, wq_ref, kvn_ref, wkv_ref, tc_ref, ts_ref,
                     qt_ref, k_ref, vt_ref):
    H = MLA_HEADS
    qscale = (MLA_QK_DIM ** -0.5) * LOG2E
    xb = x_ref[...].astype(BF16)
    lat = _dot(xb, wd_ref[...])
    o1 = MLA_Q_RANK
    o2 = o1 + MLA_KV_RANK
    cq = _rms(lat[:, :o1], qn_ref[...]).astype(BF16)
    ckv = _rms(lat[:, o1:o2], kvn_ref[...]).astype(BF16)
    tc = tc_ref[...]
    ts = ts_ref[...]
    kr = lat[:, o2:o2 + LANES] * tc + lat[:, o2 + LANES:o2 + 2 * LANES] * ts
    lane = lax.broadcasted_iota(jnp.int32, kr.shape, 1)
    kr_lo = jnp.where(lane < MLA_ROPE_DIM, kr, 0.0).astype(BF16)
    kr_hi = jnp.where(lane >= MLA_ROPE_DIM, kr, 0.0).astype(BF16)
    q = _dot(cq, wq_ref[...])
    kv = _dot(ckv, wkv_ref[...])
    r0 = H * MLA_NOPE_DIM
    r1 = r0 + H * MLA_ROPE_DIM
    for p in range(H // 2):
        qr = (q[:, r0 + LANES * p:r0 + LANES * (p + 1)] * tc
              + q[:, r1 + LANES * p:r1 + LANES * (p + 1)] * ts)
        qr_t = (qr * qscale).T.astype(BF16)
        for h in (2 * p, 2 * p + 1):
            c = QK_WIDTH * h
            qt_ref[c:c + LANES, :] = (q[:, LANES * h:LANES * (h + 1)] * qscale).T.astype(BF16)
            qt_ref[c + LANES:c + QK_WIDTH, :] = qr_t
            k_ref[:, c:c + LANES] = kv[:, c:c + LANES].astype(BF16)
            k_ref[:, c + LANES:c + QK_WIDTH] = kr_lo if h % 2 == 0 else kr_hi
            _store_vt(vt_ref, h, kv[:, c + LANES:c + QK_WIDTH])


def _mla_proj(x, wd, qn, wq, kvn, wkv, tc, ts):
    S = x.shape[0]
    tm = PROJ_ROW_TILE
    row = lambda w: pl.BlockSpec((tm, w), lambda i: (i, 0))
    full = lambda a: _resident(a.shape, lambda i: (0, 0))
    return pl.pallas_call(
        _mla_proj_kernel,
        grid=(S // tm,),
        in_specs=[row(D_MODEL), full(wd), full(qn), full(wq), full(kvn), full(wkv), row(LANES), row(LANES)],
        out_specs=_qkv_out_specs(tm),
        out_shape=_qkv_out_shapes(S),
        compiler_params=_params(1),
        name="mla_proj",
    )(x, wd, qn, wq, kvn, wkv, tc, ts)


EXP_ROWS = 16
PIPE_SLOTS = 4


def _attn_kernel(qt_ref, k_ref, vt_ref, o_ref, *scratch):
    ns = PIPE_SLOTS
    s_refs, cmax_refs, p_refs, alpha_refs = (scratch[j * ns:(j + 1) * ns] for j in range(4))
    m_ref, acc_ref = scratch[4 * ns:]
    tq = qt_ref.shape[1]
    tk = ATTN_KEY_CHUNK
    i = pl.program_id(1)
    n = (i * tq) // tk

    def chunk_start(c):
        return pl.multiple_of(jnp.clip(c, 0, n) * tk, tk)

    def stage_a(c, slot):
        s = _dot(k_ref[pl.ds(chunk_start(c), tk), :], qt_ref[...])
        s_refs[slot][...] = s
        cmax_refs[slot][...] = jnp.max(s, axis=0, keepdims=True)

    def stage_b(slot, causal_chunk=None):
        s_ref = s_refs[slot]
        if causal_chunk is None:
            cmax = cmax_refs[slot][...]
        else:
            kpos = causal_chunk * tk + lax.broadcasted_iota(jnp.int32, s_ref.shape, 0)
            qpos = i * tq + lax.broadcasted_iota(jnp.int32, s_ref.shape, 1)
            s = jnp.where(kpos <= qpos, s_ref[...], NEG_INF)
            s_ref[...] = s
            cmax = jnp.max(s, axis=0, keepdims=True)
        m_prev = m_ref[...]
        m_new = jnp.maximum(m_prev, cmax)
        alpha_refs[slot][...] = jnp.exp2(m_prev - m_new)
        m_ref[...] = m_new
        for r in range(0, tk, EXP_ROWS):
            p_refs[slot][r:r + EXP_ROWS, :] = jnp.exp2(s_ref[r:r + EXP_ROWS, :] - m_new).astype(BF16)

    def stage_c(c, slot):
        pv = _dot(vt_ref[:, pl.ds(chunk_start(c), tk)], p_refs[slot][...])
        acc_ref[...] = alpha_refs[slot][...] * acc_ref[...] + pv

    def group(u, x, y, with_c=True):
        if with_c:
            stage_c(2 * u - 2, y[0])
            stage_c(2 * u - 1, y[1])
        stage_b(x[0])
        stage_b(x[1])
        stage_a(2 * u + 2, y[0])
        stage_a(2 * u + 3, y[1])

    def finish(x, y, n_is_odd, with_c=True):
        if with_c:
            stage_c(n - n % 2 - 2, y[0])
            stage_c(n - n % 2 - 1, y[1])
        if n_is_odd:
            stage_b(x[0])
            stage_b(x[1], causal_chunk=n)
            stage_c(n - 1, x[0])
            stage_c(n, x[1])
        else:
            stage_b(x[0], causal_chunk=n)
            stage_c(n, x[0])
        acc = acc_ref[...]
        o_ref[...] = (acc[:LANES] / acc[LANES:LANES + 1]).T.astype(o_ref.dtype)

    lo, hi = (0, 1), (2, 3)
    m_ref[...] = jnp.full(m_ref.shape, -jnp.inf, F32)
    acc_ref[...] = jnp.zeros_like(acc_ref)
    stage_a(0, lo[0])
    stage_a(1, lo[1])

    @pl.when(n >= 2)
    def _():
        group(0, lo, hi, with_c=False)

    def body(u, carry):
        @pl.when(u % 2 == 0)
        def _():
            group(u, lo, hi)

        @pl.when(u % 2 == 1)
        def _():
            group(u, hi, lo)

        return carry

    lax.fori_loop(1, n // 2, body, 0)
    for r in range(2):
        @pl.when(n == r)
        def _(r=r):
            finish(lo, hi, n_is_odd=bool(r), with_c=False)

    for r in range(4):
        @pl.when((n >= 2) & (n % 4 == r))
        def _(r=r):
            x, y = (lo, hi) if r < 2 else (hi, lo)
            finish(x, y, n_is_odd=bool(r % 2))


def _attention(qt, k, vt, *, name):
    S = k.shape[0]
    tq = ATTN_Q_TILE
    tk = ATTN_KEY_CHUNK
    return pl.pallas_call(
        _attn_kernel,
        grid=(HEADS, S // tq),
        in_specs=[pl.BlockSpec((QK_WIDTH, tq), lambda h, i: (h, i)),
                  _resident((S, QK_WIDTH), lambda h, i: (0, h)),
                  _resident((V_ROWS, S), lambda h, i: (h, 0))],
        out_specs=pl.BlockSpec((tq, LANES), lambda h, i: (i, h)),
        out_shape=jax.ShapeDtypeStruct((S, LANES * HEADS), BF16),
        scratch_shapes=[pltpu.VMEM((tk, tq), F32)] * PIPE_SLOTS
        + [pltpu.VMEM((1, tq), F32)] * PIPE_SLOTS
        + [pltpu.VMEM((tk, tq), BF16)] * PIPE_SLOTS
        + [pltpu.VMEM((1, tq), F32)] * PIPE_SLOTS
        + [pltpu.VMEM((1, tq), F32),
           pltpu.VMEM((V_ROWS, tq), F32)],
        compiler_params=_params(2),
        name=name,
    )(qt, k, vt)


def _layer_norm(z, g, b):
    mu = jnp.mean(z, axis=-1, keepdims=True)
    d = z - mu
    var = jnp.mean(d * d, axis=-1, keepdims=True)
    return d * lax.rsqrt(var + LN_EPS) * g + b


def _post_kernel(x_ref, a_ref, wo_ref, win_ref, wout_ref, g1_ref, b1_ref, g2_ref, b2_ref, o_ref):
    h = _dot(a_ref[...], wo_ref[...])
    x1 = _layer_norm(DEEPNORM_ALPHA * x_ref[...] + h, g1_ref[...], b1_ref[...])
    x1b = x1.astype(BF16)
    y = None
    off = 0
    for c in FF_CHUNKS:
        gate = _dot(x1b, win_ref[:, off:off + c])
        up = _dot(x1b, win_ref[:, D_FF + off:D_FF + off + c])
        act = (gate * jax.nn.sigmoid(gate) * up).astype(BF16)
        part = _dot(act, wout_ref[off:off + c, :])
        y = part if y is None else y + part
        off += c
    o_ref[...] = _layer_norm(DEEPNORM_ALPHA * x1 + y, g2_ref[...], b2_ref[...])


def _post(x, a, wo, win, wout, g1, b1, g2, b2):
    S = x.shape[0]
    tm = FFN_ROW_TILE
    row = pl.BlockSpec((tm, D_MODEL), lambda i: (i, 0))
    full = lambda arr: _resident(arr.shape, lambda i: (0, 0))
    return pl.pallas_call(
        _post_kernel,
        grid=(S // tm,),
        in_specs=[row, row, full(wo), full(win), full(wout), full(g1), full(b1), full(g2), full(b2)],
        out_specs=row,
        out_shape=jax.ShapeDtypeStruct((S, D_MODEL), F32),
        compiler_params=_params(1),
        name="post",
    )(x, a, wo, win, wout, g1, b1, g2, b2)


def _moba_proj_kernel(x_ref, w_ref, c_ref, sa_ref, sb_ref, qt_ref, k_ref, vt_ref, kmean_ref):
    H = MOBA_HEADS
    tm = x_ref.shape[0]
    t = pl.program_id(0)
    qscale = (MOBA_HEAD_DIM ** -0.5) * LOG2E
    half = MOBA_ROT_DIM // 2

    @pl.when(t == 0)
    def _():
        kmean_ref[...] = jnp.zeros_like(kmean_ref)

    qkv = _dot(x_ref[...].astype(BF16), w_ref[...])
    cc = c_ref[...]
    sa = sa_ref[...]
    sb = sb_ref[...]

    def rope(z):
        return z * cc + pltpu.roll(z, LANES - half, 1) * sa + pltpu.roll(z, half, 1) * sb

    blocks = tm // MOBA_BLOCK
    nb = kmean_ref.shape[0]
    nblk = lax.broadcasted_iota(jnp.int32, (nb, tm), 0).astype(F32)
    qpos = t * tm + lax.broadcasted_iota(jnp.int32, (nb, tm), 1)
    own = (qpos // MOBA_BLOCK).astype(F32)
    past = nblk < own
    kmean_row = lax.broadcasted_iota(jnp.int32, (nb, LANES), 0)
    key_blk = (t * tm + lax.broadcasted_iota(jnp.int32, (tm, LANES), 0)) // MOBA_BLOCK
    blk_onehot = (key_blk == lax.broadcasted_iota(jnp.int32, (tm, LANES), 1)).astype(BF16)
    never = jnp.full((LANES - nb, tm), NEG_INF, BF16)
    for h in range(H):
        c = LANES * h
        qh = rope(qkv[:, c:c + LANES])
        kh = rope(qkv[:, H * LANES + c:H * LANES + c + LANES])
        k_ref[:, 2 * c:2 * c + LANES] = kh.astype(BF16)
        k_ref[:, 2 * c + LANES:2 * c + QK_WIDTH] = blk_onehot
        _store_vt(vt_ref, h, qkv[:, 2 * H * LANES + c:2 * H * LANES + c + LANES])
        km = kmean_ref[:, c:c + LANES]
        for b in range(blocks):
            mean_b = jnp.mean(kh[b * MOBA_BLOCK:(b + 1) * MOBA_BLOCK], axis=0, keepdims=True)
            km = jnp.where(kmean_row == t * blocks + b, mean_b, km)
        kmean_ref[:, c:c + LANES] = km
        km_hi = km.astype(BF16)
        km_lo = (km - km_hi.astype(F32)).astype(BF16)
        q_hi = qh.astype(BF16)
        q_lo = (qh - q_hi.astype(F32)).astype(BF16)
        g = _dot_nt(km_hi, q_hi) + (_dot_nt(km_hi, q_lo) + _dot_nt(km_lo, q_hi))
        g = jnp.where(past, g, NEG_INF)
        for _ in range(MOBA_TOPK):
            best = jnp.max(g, axis=0, keepdims=True)
            idx = jnp.min(jnp.where(g == best, nblk, float(nb)), axis=0, keepdims=True)
            g = jnp.where(nblk == idx, -jnp.inf, g)
        sel = ((g == -jnp.inf) & past) | (nblk == own)
        qt_ref[2 * c:2 * c + LANES, :] = (qh * qscale).T.astype(BF16)
        qt_ref[2 * c + LANES:2 * c + LANES + nb, :] = jnp.where(sel, 0.0, NEG_INF).astype(BF16)
        qt_ref[2 * c + LANES + nb:2 * c + QK_WIDTH, :] = never


def _moba_proj(x, w, cc, sa, sb):
    S = x.shape[0]
    tm = PROJ_ROW_TILE
    row = lambda wd: pl.BlockSpec((tm, wd), lambda i: (i, 0))
    return pl.pallas_call(
        _moba_proj_kernel,
        grid=(S // tm,),
        in_specs=[row(D_MODEL), _resident(w.shape, lambda i: (0, 0)), row(LANES), row(LANES), row(LANES)],
        out_specs=_qkv_out_specs(tm),
        out_shape=_qkv_out_shapes(S),
        scratch_shapes=[pltpu.VMEM((S // MOBA_BLOCK, LANES * MOBA_HEADS), F32)],
        compiler_params=_params(1),
        name="moba_proj",
    )(x, w, cc, sa, sb)


def _rotary_tables(seq_len, rot_dim):
    inv_freq = ROPE_THETA ** (-jnp.arange(0, rot_dim, 2, dtype=F32) / rot_dim)
    lo = jnp.arange(LANES, dtype=F32)[:, None] * inv_freq[None, :]
    hi = (jnp.arange(seq_len // LANES, dtype=F32) * LANES)[:, None] * inv_freq[None, :]
    cl, sl, ch, sh = jnp.cos(lo)[None], jnp.sin(lo)[None], jnp.cos(hi)[:, None], jnp.sin(hi)[:, None]
    cos = (ch * cl - sh * sl).reshape(seq_len, rot_dim // 2)
    sin = (sh * cl + ch * sl).reshape(seq_len, rot_dim // 2)
    return cos, sin


def _swap_halves(w, width):
    shp = w.shape
    w = w.reshape(shp[0], -1, 2, width // 2)
    return w[:, :, ::-1, :].reshape(shp)


def kernel(x, mla_w_dqkv, mla_q_norm, mla_w_uq, mla_kv_norm, mla_w_ukv, mla_w_o, moba_w_qkv, moba_w_o,
           ffn_w_in, ffn_w_out, ln_mix_g, ln_mix_b, ln_ffn_g, ln_ffn_b):
    B, S, D = x.shape
    assert B == 1 and D == D_MODEL and S % ATTN_KEY_CHUNK == 0 and S // MOBA_BLOCK <= LANES and (S // MOBA_BLOCK) % BF16_SUBLANES == 0
    assert MLA_HEADS == HEADS and MOBA_HEADS == HEADS and ATTN_KEY_CHUNK % ATTN_Q_TILE == 0
    xs = x[0]
    H = MLA_HEADS

    cos, sin = _rotary_tables(S, MLA_ROPE_DIM)
    tc = jnp.concatenate([cos] * 4, axis=1)
    ts = jnp.concatenate([-sin, sin] * 2, axis=1)
    cos, sin = _rotary_tables(S, MOBA_ROT_DIM)
    hr = MOBA_ROT_DIM // 2
    rest = LANES - MOBA_ROT_DIM
    mc = jnp.concatenate([cos, cos, jnp.ones((S, rest), F32)], axis=1)
    msa = jnp.concatenate([-sin, jnp.zeros((S, LANES - hr), F32)], axis=1)
    msb = jnp.concatenate([jnp.zeros((S, hr), F32), sin, jnp.zeros((S, rest), F32)], axis=1)

    wd = mla_w_dqkv[0]
    w_kr = wd[:, MLA_Q_RANK + MLA_KV_RANK:]
    w_kr_sw = _swap_halves(w_kr, MLA_ROPE_DIM)
    wd_ext = jnp.concatenate([wd[:, :MLA_Q_RANK + MLA_KV_RANK], w_kr, w_kr, w_kr_sw, w_kr_sw], axis=1).astype(BF16)
    wq = mla_w_uq[0].reshape(MLA_Q_RANK, H, MLA_QK_DIM)
    wq_nope = wq[:, :, :MLA_NOPE_DIM].reshape(MLA_Q_RANK, H * MLA_NOPE_DIM)
    wq_rope = wq[:, :, MLA_NOPE_DIM:].reshape(MLA_Q_RANK, H * MLA_ROPE_DIM)
    wq_ext = jnp.concatenate([wq_nope, wq_rope, _swap_halves(wq_rope, MLA_ROPE_DIM)], axis=1).astype(BF16)

    qt, k, vt = _mla_proj(xs, wd_ext, mla_q_norm[0][None, :], wq_ext, mla_kv_norm[0][None, :],
                          mla_w_ukv[0].astype(BF16), tc, ts)
    a = _attention(qt, k, vt, name="mla_attn")
    x1 = _post(xs, a, mla_w_o[0].astype(BF16), ffn_w_in[0].astype(BF16), ffn_w_out[0].astype(BF16),
               ln_mix_g[0][None, :], ln_mix_b[0][None, :], ln_ffn_g[0][None, :], ln_ffn_b[0][None, :])

    qt, k, vt = _moba_proj(x1, moba_w_qkv[0].astype(BF16), mc, msa, msb)
    a = _attention(qt, k, vt, name="moba_attn")
    x2 = _post(x1, a, moba_w_o[0].astype(BF16), ffn_w_in[1].astype(BF16), ffn_w_out[1].astype(BF16),
               ln_mix_g[1][None, :], ln_mix_b[1][None, :], ln_ffn_g[1][None, :], ln_ffn_b[1][None, :])
    return x2[None]
```

```python
import math

import jax
import jax.numpy as jnp
from jax import lax
from jax.experimental import pallas as pl
from jax.experimental.pallas import tpu as pltpu

D_MODEL = 1024
DEPTH = 2
ROPE_THETA = 500000.0
NEG_INF = -1e30
LN_EPS = 1e-5
RMS_EPS = 1e-6
MLA_HEADS = 8
MLA_Q_RANK = 384
MLA_KV_RANK = 256
MLA_NOPE_DIM = 128
MLA_ROPE_DIM = 64
MLA_V_DIM = 128
MLA_QK_DIM = MLA_NOPE_DIM + MLA_ROPE_DIM
MOBA_HEADS = 8
MOBA_HEAD_DIM = D_MODEL // MOBA_HEADS
MOBA_ROT_DIM = MOBA_HEAD_DIM // 4
MOBA_BLOCK = 256
MOBA_TOPK = 3
D_FF = 2816
DEEPNORM_ALPHA = (2 * DEPTH) ** 0.25

LANES = 128
BF16_SUBLANES = 16
LOG2E = math.log2(math.e)

HEADS = 8
QK_WIDTH = 2 * LANES
V_ROWS = LANES + BF16_SUBLANES
PROJ_ROW_TILE = 512
FFN_ROW_TILE = 512
ATTN_Q_TILE = 512
ATTN_KEY_CHUNK = 1024
FF_CHUNKS = (768, 768, 768, 512)
VMEM_LIMIT = 48 * 1024 * 1024

BF16 = jnp.bfloat16
F32 = jnp.float32


def _dot(a, b, precision=None):
    return jnp.dot(a, b, preferred_element_type=F32, precision=precision)


def _dot_nt(a, b, precision=None):
    return lax.dot_general(a, b, (((1,), (1,)), ((), ())), preferred_element_type=F32, precision=precision)


def _resident(shape, index_map):
    return pl.BlockSpec(shape, index_map, pipeline_mode=pl.Buffered(1))


def _params(n_grid_dims):
    return pltpu.CompilerParams(dimension_semantics=("arbitrary",) * n_grid_dims, vmem_limit_bytes=VMEM_LIMIT)


def _qkv_out_specs(tm):
    return [pl.BlockSpec((HEADS * QK_WIDTH, tm), lambda i: (0, i)),
            pl.BlockSpec((tm, HEADS * QK_WIDTH), lambda i: (i, 0)),
            pl.BlockSpec((HEADS * V_ROWS, tm), lambda i: (0, i))]


def _qkv_out_shapes(S):
    return [jax.ShapeDtypeStruct((HEADS * QK_WIDTH, S), BF16),
            jax.ShapeDtypeStruct((S, HEADS * QK_WIDTH), BF16),
            jax.ShapeDtypeStruct((HEADS * V_ROWS, S), BF16)]


def _store_vt(vt_ref, h, v):
    tm = v.shape[0]
    r = V_ROWS * h
    vt_ref[r:r + LANES, :] = v.T.astype(BF16)
    pad_row = lax.broadcasted_iota(jnp.int32, (BF16_SUBLANES, tm), 0)
    vt_ref[r + LANES:r + V_ROWS, :] = (pad_row == 0).astype(BF16)


def _rms(x, g):
    return x * lax.rsqrt(jnp.mean(x * x, axis=-1, keepdims=True) + RMS_EPS) * g


def _angle_table(ch_ref, sh_ref, p_ref, q_ref):
    p = p_ref[...]
    q = q_ref[...]
    return jnp.concatenate([ch_ref[0, g:g + 1, :] * p + sh_ref[0, g:g + 1, :] * q
                            for g in range(ch_ref.shape[1])], axis=0)


def _mla_proj_kernel(x_ref, wd_ref, qn_ref, wq_ref, kvn_ref, wkv_ref, ch_ref, sh_ref, pc_ref, qc_ref, ps_ref, qs_ref,
                     qt_ref, k_ref, vt_ref):
    H = MLA_HEADS
    qscale = (MLA_QK_DIM ** -0.5) * LOG2E
    xb = x_ref[...].astype(BF16)
    lat = _dot(xb, wd_ref[...])
    o1 = MLA_Q_RANK
    o2 = o1 + MLA_KV_RANK
    cq = _rms(lat[:, :o1], qn_ref[...]).astype(BF16)
    ckv = _rms(lat[:, o1:o2], kvn_ref[...]).astype(BF16)
    tc = _angle_table(ch_ref, sh_ref, pc_ref, qc_ref)
    ts = _angle_table(ch_ref, sh_ref, ps_ref, qs_ref)
    kr = lat[:, o2:o2 + LANES] * tc + lat[:, o2 + LANES:o2 + 2 * LANES] * ts
    lane = lax.broadcasted_iota(jnp.int32, kr.shape, 1)
    kr_lo = jnp.where(lane < MLA_ROPE_DIM, kr, 0.0).astype(BF16)
    kr_hi = jnp.where(lane >= MLA_ROPE_DIM, kr, 0.0).astype(BF16)
    q = _dot(cq, wq_ref[...])
    kv = _dot(ckv, wkv_ref[...])
    r0 = H * MLA_NOPE_DIM
    r1 = r0 + H * MLA_ROPE_DIM
    for p in range(H // 2):
        qr = (q[:, r0 + LANES * p:r0 + LANES * (p + 1)] * tc
              + q[:, r1 + LANES * p:r1 + LANES * (p + 1)] * ts)
        qr_t = (qr * qscale).T.astype(BF16)
        for h in (2 * p, 2 * p + 1):
            c = QK_WIDTH * h
            qt_ref[c:c + LANES, :] = (q[:, LANES * h:LANES * (h + 1)] * qscale).T.astype(BF16)
            qt_ref[c + LANES:c + QK_WIDTH, :] = qr_t
            k_ref[:, c:c + LANES] = kv[:, c:c + LANES].astype(BF16)
            k_ref[:, c + LANES:c + QK_WIDTH] = kr_lo if h % 2 == 0 else kr_hi
            _store_vt(vt_ref, h, kv[:, c + LANES:c + QK_WIDTH])


def _hi_spec(tm):
    return pl.BlockSpec((1, tm // LANES, LANES), lambda i: (i, 0, 0))


def _mla_proj(x, wd, qn, wq, kvn, wkv, ch, sh, lo_tables):
    S = x.shape[0]
    tm = PROJ_ROW_TILE
    row = lambda w: pl.BlockSpec((tm, w), lambda i: (i, 0))
    full = lambda a: _resident(a.shape, lambda i: (0, 0))
    return pl.pallas_call(
        _mla_proj_kernel,
        grid=(S // tm,),
        in_specs=[row(D_MODEL), full(wd), full(qn), full(wq), full(kvn), full(wkv), _hi_spec(tm), _hi_spec(tm)]
        + [full(t) for t in lo_tables],
        out_specs=_qkv_out_specs(tm),
        out_shape=_qkv_out_shapes(S),
        compiler_params=_params(1),
        name="mla_proj",
    )(x, wd, qn, wq, kvn, wkv, ch, sh, *lo_tables)


EXP_ROWS = 16
PIPE_SLOTS = 4


def _attn_kernel(qt_ref, k_ref, vt_ref, o_ref, *scratch):
    ns = PIPE_SLOTS
    s_refs, cmax_refs, p_refs, alpha_refs = (scratch[j * ns:(j + 1) * ns] for j in range(4))
    m_ref, acc_ref = scratch[4 * ns:]
    tq = qt_ref.shape[1]
    tk = ATTN_KEY_CHUNK
    i = pl.program_id(1)
    n = (i * tq) // tk

    def chunk_start(c):
        return pl.multiple_of(jnp.clip(c, 0, n) * tk, tk)

    def stage_a(c, slot):
        s = _dot(k_ref[pl.ds(chunk_start(c), tk), :], qt_ref[...])
        s_refs[slot][...] = s
        cmax_refs[slot][...] = jnp.max(s, axis=0, keepdims=True)

    def stage_b(slot, causal_chunk=None):
        s_ref = s_refs[slot]
        if causal_chunk is None:
            cmax = cmax_refs[slot][...]
        else:
            kpos = causal_chunk * tk + lax.broadcasted_iota(jnp.int32, s_ref.shape, 0)
            qpos = i * tq + lax.broadcasted_iota(jnp.int32, s_ref.shape, 1)
            s = jnp.where(kpos <= qpos, s_ref[...], NEG_INF)
            s_ref[...] = s
            cmax = jnp.max(s, axis=0, keepdims=True)
        m_prev = m_ref[...]
        m_new = jnp.maximum(m_prev, cmax)
        alpha_refs[slot][...] = jnp.exp2(m_prev - m_new)
        m_ref[...] = m_new
        for r in range(0, tk, EXP_ROWS):
            p_refs[slot][r:r + EXP_ROWS, :] = jnp.exp2(s_ref[r:r + EXP_ROWS, :] - m_new).astype(BF16)

    def stage_c(c, slot):
        pv = _dot(vt_ref[:, pl.ds(chunk_start(c), tk)], p_refs[slot][...])
        acc_ref[...] = alpha_refs[slot][...] * acc_ref[...] + pv

    def group(u, x, y, with_c=True):
        if with_c:
            stage_c(2 * u - 2, y[0])
            stage_c(2 * u - 1, y[1])
        stage_b(x[0])
        stage_b(x[1])
        stage_a(2 * u + 2, y[0])
        stage_a(2 * u + 3, y[1])

    def finish(x, y, n_is_odd, with_c=True):
        if with_c:
            stage_c(n - n % 2 - 2, y[0])
            stage_c(n - n % 2 - 1, y[1])
        if n_is_odd:
            stage_b(x[0])
            stage_b(x[1], causal_chunk=n)
            stage_c(n - 1, x[0])
            stage_c(n, x[1])
        else:
            stage_b(x[0], causal_chunk=n)
            stage_c(n, x[0])
        acc = acc_ref[...]
        o_ref[...] = (acc[:LANES] / acc[LANES:LANES + 1]).T.astype(o_ref.dtype)

    lo, hi = (0, 1), (2, 3)
    m_ref[...] = jnp.full(m_ref.shape, -jnp.inf, F32)
    acc_ref[...] = jnp.zeros_like(acc_ref)
    stage_a(0, lo[0])
    stage_a(1, lo[1])

    @pl.when(n >= 2)
    def _():
        group(0, lo, hi, with_c=False)

    def body(u, carry):
        @pl.when(u % 2 == 0)
        def _():
            group(u, lo, hi)

        @pl.when(u % 2 == 1)
        def _():
            group(u, hi, lo)

        return carry

    lax.fori_loop(1, n // 2, body, 0)
    for r in range(2):
        @pl.when(n == r)
        def _(r=r):
            finish(lo, hi, n_is_odd=bool(r), with_c=False)

    for r in range(4):
        @pl.when((n >= 2) & (n % 4 == r))
        def _(r=r):
            x, y = (lo, hi) if r < 2 else (hi, lo)
            finish(x, y, n_is_odd=bool(r % 2))


def _attention(qt, k, vt, *, name):
    S = k.shape[0]
    tq = ATTN_Q_TILE
    tk = ATTN_KEY_CHUNK
    return pl.pallas_call(
        _attn_kernel,
        grid=(HEADS, S // tq),
        in_specs=[pl.BlockSpec((QK_WIDTH, tq), lambda h, i: (h, i)),
                  pl.BlockSpec((S, QK_WIDTH), lambda h, i: (0, h)),
                  pl.BlockSpec((V_ROWS, S), lambda h, i: (h, 0))],
        out_specs=pl.BlockSpec((tq, LANES), lambda h, i: (i, h)),
        out_shape=jax.ShapeDtypeStruct((S, LANES * HEADS), BF16),
        scratch_shapes=[pltpu.VMEM((tk, tq), F32)] * PIPE_SLOTS
        + [pltpu.VMEM((1, tq), F32)] * PIPE_SLOTS
        + [pltpu.VMEM((tk, tq), BF16)] * PIPE_SLOTS
        + [pltpu.VMEM((1, tq), F32)] * PIPE_SLOTS
        + [pltpu.VMEM((1, tq), F32),
           pltpu.VMEM((V_ROWS, tq), F32)],
        compiler_params=_params(2),
        name=name,
    )(qt, k, vt)


def _layer_norm(z, g, b):
    mu = jnp.mean(z, axis=-1, keepdims=True)
    d = z - mu
    var = jnp.mean(d * d, axis=-1, keepdims=True)
    return d * lax.rsqrt(var + LN_EPS) * g + b


def _post_kernel(x_ref, a_ref, wo_ref, win_ref, wout_ref, g1_ref, b1_ref, g2_ref, b2_ref, o_ref):
    h = _dot(a_ref[...], wo_ref[...])
    x1 = _layer_norm(DEEPNORM_ALPHA * x_ref[...] + h, g1_ref[...], b1_ref[...])
    x1b = x1.astype(BF16)
    y = None
    off = 0
    for c in FF_CHUNKS:
        gate = _dot(x1b, win_ref[:, off:off + c])
        up = _dot(x1b, win_ref[:, D_FF + off:D_FF + off + c])
        act = (gate * jax.nn.sigmoid(gate) * up).astype(BF16)
        part = _dot(act, wout_ref[off:off + c, :])
        y = part if y is None else y + part
        off += c
    o_ref[...] = _layer_norm(DEEPNORM_ALPHA * x1 + y, g2_ref[...], b2_ref[...])


def _post(x, a, wo, win, wout, g1, b1, g2, b2):
    S = x.shape[0]
    tm = FFN_ROW_TILE
    row = pl.BlockSpec((tm, D_MODEL), lambda i: (i, 0))
    full = lambda arr: _resident(arr.shape, lambda i: (0, 0))
    return pl.pallas_call(
        _post_kernel,
        grid=(S // tm,),
        in_specs=[row, row, full(wo), full(win), full(wout), full(g1), full(b1), full(g2), full(b2)],
        out_specs=row,
        out_shape=jax.ShapeDtypeStruct((S, D_MODEL), F32),
        compiler_params=_params(1),
        name="post",
    )(x, a, wo, win, wout, g1, b1, g2, b2)


def _moba_proj_kernel(x_ref, w_ref, ch_ref, sh_ref, pc_ref, qc_ref, pa_ref, qa_ref, pb_ref, qb_ref,
                      qt_ref, k_ref, vt_ref, kmean_ref):
    H = MOBA_HEADS
    tm = x_ref.shape[0]
    t = pl.program_id(0)
    qscale = (MOBA_HEAD_DIM ** -0.5) * LOG2E
    half = MOBA_ROT_DIM // 2

    @pl.when(t == 0)
    def _():
        kmean_ref[...] = jnp.zeros_like(kmean_ref)

    qkv = _dot(x_ref[...].astype(BF16), w_ref[...])
    cc = _angle_table(ch_ref, sh_ref, pc_ref, qc_ref)
    sa = _angle_table(ch_ref, sh_ref, pa_ref, qa_ref)
    sb = _angle_table(ch_ref, sh_ref, pb_ref, qb_ref)

    def rope(z):
        return z * cc + pltpu.roll(z, LANES - half, 1) * sa + pltpu.roll(z, half, 1) * sb

    blocks = tm // MOBA_BLOCK
    nb = kmean_ref.shape[0]
    nblk = lax.broadcasted_iota(jnp.int32, (nb, tm), 0).astype(F32)
    qpos = t * tm + lax.broadcasted_iota(jnp.int32, (nb, tm), 1)
    own = (qpos // MOBA_BLOCK).astype(F32)
    past = nblk < own
    kmean_row = lax.broadcasted_iota(jnp.int32, (nb, LANES), 0)
    key_blk = (t * tm + lax.broadcasted_iota(jnp.int32, (tm, LANES), 0)) // MOBA_BLOCK
    blk_onehot = (key_blk == lax.broadcasted_iota(jnp.int32, (tm, LANES), 1)).astype(BF16)
    never = jnp.full((LANES - nb, tm), NEG_INF, BF16)
    for h in range(H):
        c = LANES * h
        qh = rope(qkv[:, c:c + LANES])
        kh = rope(qkv[:, H * LANES + c:H * LANES + c + LANES])
        k_ref[:, 2 * c:2 * c + LANES] = kh.astype(BF16)
        k_ref[:, 2 * c + LANES:2 * c + QK_WIDTH] = blk_onehot
        _store_vt(vt_ref, h, qkv[:, 2 * H * LANES + c:2 * H * LANES + c + LANES])
        km = kmean_ref[:, c:c + LANES]
        for b in range(blocks):
            mean_b = jnp.mean(kh[b * MOBA_BLOCK:(b + 1) * MOBA_BLOCK], axis=0, keepdims=True)
            km = jnp.where(kmean_row == t * blocks + b, mean_b, km)
        kmean_ref[:, c:c + LANES] = km
        km_hi = km.astype(BF16)
        km_lo = (km - km_hi.astype(F32)).astype(BF16)
        q_hi = qh.astype(BF16)
        q_lo = (qh - q_hi.astype(F32)).astype(BF16)
        g = _dot_nt(km_hi, q_hi) + (_dot_nt(km_hi, q_lo) + _dot_nt(km_lo, q_hi))
        g = jnp.where(past, g, NEG_INF)
        for _ in range(MOBA_TOPK):
            best = jnp.max(g, axis=0, keepdims=True)
            idx = jnp.min(jnp.where(g == best, nblk, float(nb)), axis=0, keepdims=True)
            g = jnp.where(nblk == idx, -jnp.inf, g)
        sel = ((g == -jnp.inf) & past) | (nblk == own)
        qt_ref[2 * c:2 * c + LANES, :] = (qh * qscale).T.astype(BF16)
        qt_ref[2 * c + LANES:2 * c + LANES + nb, :] = jnp.where(sel, 0.0, NEG_INF).astype(BF16)
        qt_ref[2 * c + LANES + nb:2 * c + QK_WIDTH, :] = never


def _moba_proj(x, w, ch, sh, lo_tables):
    S = x.shape[0]
    tm = PROJ_ROW_TILE
    row = lambda wd: pl.BlockSpec((tm, wd), lambda i: (i, 0))
    return pl.pallas_call(
        _moba_proj_kernel,
        grid=(S // tm,),
        in_specs=[row(D_MODEL), _resident(w.shape, lambda i: (0, 0)), _hi_spec(tm), _hi_spec(tm)]
        + [_resident(t.shape, lambda i: (0, 0)) for t in lo_tables],
        out_specs=_qkv_out_specs(tm),
        out_shape=_qkv_out_shapes(S),
        scratch_shapes=[pltpu.VMEM((S // MOBA_BLOCK, LANES * MOBA_HEADS), F32)],
        compiler_params=_params(1),
        name="moba_proj",
    )(x, w, ch, sh, *lo_tables)


def _rotary_factors(seq_len, tm, lane_freq, combos):
    lo = jnp.arange(LANES, dtype=F32)[:, None] * lane_freq[None, :]
    hi = (jnp.arange(seq_len // LANES, dtype=F32) * LANES)[:, None] * lane_freq[None, :]
    cl, sl = jnp.cos(lo), jnp.sin(lo)
    shape = (seq_len // tm, tm // LANES, LANES)
    lo_tables = []
    for a, b in combos:
        lo_tables += [a[None, :] * cl + b[None, :] * sl, b[None, :] * cl - a[None, :] * sl]
    return jnp.cos(hi).reshape(shape), jnp.sin(hi).reshape(shape), lo_tables


def _swap_halves(w, width):
    shp = w.shape
    w = w.reshape(shp[0], -1, 2, width // 2)
    return w[:, :, ::-1, :].reshape(shp)


def kernel(x, mla_w_dqkv, mla_q_norm, mla_w_uq, mla_kv_norm, mla_w_ukv, mla_w_o, moba_w_qkv, moba_w_o,
           ffn_w_in, ffn_w_out, ln_mix_g, ln_mix_b, ln_ffn_g, ln_ffn_b):
    B, S, D = x.shape
    assert B == 1 and D == D_MODEL and S % ATTN_KEY_CHUNK == 0 and S // MOBA_BLOCK <= LANES and (S // MOBA_BLOCK) % BF16_SUBLANES == 0
    assert MLA_HEADS == HEADS and MOBA_HEADS == HEADS and ATTN_KEY_CHUNK % ATTN_Q_TILE == 0
    xs = x[0]
    H = MLA_HEADS

    lane = jnp.arange(LANES)
    zero, one = jnp.zeros((LANES,), F32), jnp.ones((LANES,), F32)
    f_mla = ROPE_THETA ** (-jnp.arange(0, MLA_ROPE_DIM, 2, dtype=F32) / MLA_ROPE_DIM)
    hr = MLA_ROPE_DIM // 2
    sign = jnp.where(lane % MLA_ROPE_DIM < hr, -1.0, 1.0).astype(F32)
    mla_tables = _rotary_factors(S, PROJ_ROW_TILE, f_mla[lane % hr], [(one, zero), (zero, sign)])
    f_moba = ROPE_THETA ** (-jnp.arange(0, MOBA_ROT_DIM, 2, dtype=F32) / MOBA_ROT_DIM)
    hr = MOBA_ROT_DIM // 2
    rot = lane < MOBA_ROT_DIM
    first = (lane < hr).astype(F32)
    second = (rot & (lane >= hr)).astype(F32)
    moba_tables = _rotary_factors(S, PROJ_ROW_TILE, jnp.where(rot, f_moba[lane % hr], 0.0),
                                  [(one, zero), (zero, -first), (zero, second)])

    wd = mla_w_dqkv[0]
    w_kr = wd[:, MLA_Q_RANK + MLA_KV_RANK:]
    w_kr_sw = _swap_halves(w_kr, MLA_ROPE_DIM)
    wd_ext = jnp.concatenate([wd[:, :MLA_Q_RANK + MLA_KV_RANK], w_kr, w_kr, w_kr_sw, w_kr_sw], axis=1).astype(BF16)
    wq = mla_w_uq[0].reshape(MLA_Q_RANK, H, MLA_QK_DIM)
    wq_nope = wq[:, :, :MLA_NOPE_DIM].reshape(MLA_Q_RANK, H * MLA_NOPE_DIM)
    wq_rope = wq[:, :, MLA_NOPE_DIM:].reshape(MLA_Q_RANK, H * MLA_ROPE_DIM)
    wq_ext = jnp.concatenate([wq_nope, wq_rope, _swap_halves(wq_rope, MLA_ROPE_DIM)], axis=1).astype(BF16)

    qt, k, vt = _mla_proj(xs, wd_ext, mla_q_norm[0][None, :], wq_ext, mla_kv_norm[0][None, :],
                          mla_w_ukv[0].astype(BF16), *mla_tables)
    a = _attention(qt, k, vt, name="mla_attn")
    x1 = _post(xs, a, mla_w_o[0].astype(BF16), ffn_w_in[0].astype(BF16), ffn_w_out[0].astype(BF16),
               ln_mix_g[0][None, :], ln_mix_b[0][None, :], ln_ffn_g[0][None, :], ln_ffn_b[0][None, :])

    qt, k, vt = _moba_proj(x1, moba_w_qkv[0].astype(BF16), *moba_tables)
    a = _attention(qt, k, vt, name="moba_attn")
    x2 = _post(x1, a, moba_w_o[0].astype(BF16), ffn_w_in[1].astype(BF16), ffn_w_out[1].astype(BF16),
               ln_mix_g[1][None, :], ln_mix_b[1][None, :], ln_ffn_g[1][None, :], ln_ffn_b[1][None, :])
    return x2[None]
```

```python
import math

import jax
import jax.numpy as jnp
from jax import lax
from jax.experimental import pallas as pl
from jax.experimental.pallas import tpu as pltpu

D_MODEL = 1024
DEPTH = 2
ROPE_THETA = 500000.0
NEG_INF = -1e30
LN_EPS = 1e-5
RMS_EPS = 1e-6
MLA_HEADS = 8
MLA_Q_RANK = 384
MLA_KV_RANK = 256
MLA_NOPE_DIM = 128
MLA_ROPE_DIM = 64
MLA_V_DIM = 128
MLA_QK_DIM = MLA_NOPE_DIM + MLA_ROPE_DIM
MOBA_HEADS = 8
MOBA_HEAD_DIM = D_MODEL // MOBA_HEADS
MOBA_ROT_DIM = MOBA_HEAD_DIM // 4
MOBA_BLOCK = 256
MOBA_TOPK = 3
D_FF = 2816
DEEPNORM_ALPHA = (2 * DEPTH) ** 0.25

LANES = 128
BF16_SUBLANES = 16
LOG2E = math.log2(math.e)

HEADS = 8
QK_WIDTH = 2 * LANES
V_ROWS = LANES + BF16_SUBLANES
PROJ_ROW_TILE = 512
FFN_ROW_TILE = 512
ATTN_Q_TILE = 1024
ATTN_KEY_CHUNK = 1024
FF_CHUNKS = (768, 768, 768, 512)
VMEM_LIMIT = 56 * 1024 * 1024

BF16 = jnp.bfloat16
F32 = jnp.float32


def _dot(a, b, precision=None):
    return jnp.dot(a, b, preferred_element_type=F32, precision=precision)


def _dot_nt(a, b, precision=None):
    return lax.dot_general(a, b, (((1,), (1,)), ((), ())), preferred_element_type=F32, precision=precision)


def _resident(shape, index_map):
    return pl.BlockSpec(shape, index_map, pipeline_mode=pl.Buffered(1))


def _params(n_grid_dims):
    return pltpu.CompilerParams(dimension_semantics=("arbitrary",) * n_grid_dims, vmem_limit_bytes=VMEM_LIMIT)


def _qkv_out_specs(tm):
    return [pl.BlockSpec((HEADS * QK_WIDTH, tm), lambda i: (0, i)),
            pl.BlockSpec((tm, HEADS * QK_WIDTH), lambda i: (i, 0)),
            pl.BlockSpec((HEADS * V_ROWS, tm), lambda i: (0, i))]


def _qkv_out_shapes(S):
    return [jax.ShapeDtypeStruct((HEADS * QK_WIDTH, S), BF16),
            jax.ShapeDtypeStruct((S, HEADS * QK_WIDTH), BF16),
            jax.ShapeDtypeStruct((HEADS * V_ROWS, S), BF16)]


def _store_vt(vt_ref, h, v):
    tm = v.shape[0]
    r = V_ROWS * h
    vt_ref[r:r + LANES, :] = v.T.astype(BF16)
    pad_row = lax.broadcasted_iota(jnp.int32, (BF16_SUBLANES, tm), 0)
    vt_ref[r + LANES:r + V_ROWS, :] = (pad_row == 0).astype(BF16)


def _rms(x, g):
    return x * lax.rsqrt(jnp.mean(x * x, axis=-1, keepdims=True) + RMS_EPS) * g


def _angle_table(ch_ref, sh_ref, p_ref, q_ref):
    p = p_ref[...]
    q = q_ref[...]
    return jnp.concatenate([ch_ref[0, g:g + 1, :] * p + sh_ref[0, g:g + 1, :] * q
                            for g in range(ch_ref.shape[1])], axis=0)


def _mla_proj_kernel(x_ref, wd_ref, qn_ref, wq_ref, kvn_ref, wkv_ref, ch_ref, sh_ref, pc_ref, qc_ref, ps_ref, qs_ref,
                     qt_ref, k_ref, vt_ref):
    H = MLA_HEADS
    qscale = (MLA_QK_DIM ** -0.5) * LOG2E
    xb = x_ref[...].astype(BF16)
    lat = _dot(xb, wd_ref[...])
    o1 = MLA_Q_RANK
    o2 = o1 + MLA_KV_RANK
    cq = _rms(lat[:, :o1], qn_ref[...]).astype(BF16)
    ckv = _rms(lat[:, o1:o2], kvn_ref[...]).astype(BF16)
    tc = _angle_table(ch_ref, sh_ref, pc_ref, qc_ref)
    ts = _angle_table(ch_ref, sh_ref, ps_ref, qs_ref)
    kr = lat[:, o2:o2 + LANES] * tc + lat[:, o2 + LANES:o2 + 2 * LANES] * ts
    lane = lax.broadcasted_iota(jnp.int32, kr.shape, 1)
    kr_lo = jnp.where(lane < MLA_ROPE_DIM, kr, 0.0).astype(BF16)
    kr_hi = jnp.where(lane >= MLA_ROPE_DIM, kr, 0.0).astype(BF16)
    q = _dot(cq, wq_ref[...])
    kv = _dot(ckv, wkv_ref[...])
    r0 = H * MLA_NOPE_DIM
    r1 = r0 + H * MLA_ROPE_DIM
    for p in range(H // 2):
        qr = (q[:, r0 + LANES * p:r0 + LANES * (p + 1)] * tc
              + q[:, r1 + LANES * p:r1 + LANES * (p + 1)] * ts)
        qr_t = (qr * qscale).T.astype(BF16)
        for h in (2 * p, 2 * p + 1):
            c = QK_WIDTH * h
            qt_ref[c:c + LANES, :] = (q[:, LANES * h:LANES * (h + 1)] * qscale).T.astype(BF16)
            qt_ref[c + LANES:c + QK_WIDTH, :] = qr_t
            k_ref[:, c:c + LANES] = kv[:, c:c + LANES].astype(BF16)
            k_ref[:, c + LANES:c + QK_WIDTH] = kr_lo if h % 2 == 0 else kr_hi
            _store_vt(vt_ref, h, kv[:, c + LANES:c + QK_WIDTH])


def _hi_spec(tm):
    return pl.BlockSpec((1, tm // LANES, LANES), lambda i: (i, 0, 0))


def _mla_proj(x, wd, qn, wq, kvn, wkv, ch, sh, lo_tables):
    S = x.shape[0]
    tm = PROJ_ROW_TILE
    row = lambda w: pl.BlockSpec((tm, w), lambda i: (i, 0))
    full = lambda a: _resident(a.shape, lambda i: (0, 0))
    return pl.pallas_call(
        _mla_proj_kernel,
        grid=(S // tm,),
        in_specs=[row(D_MODEL), full(wd), full(qn), full(wq), full(kvn), full(wkv), _hi_spec(tm), _hi_spec(tm)]
        + [full(t) for t in lo_tables],
        out_specs=_qkv_out_specs(tm),
        out_shape=_qkv_out_shapes(S),
        compiler_params=_params(1),
        name="mla_proj",
    )(x, wd, qn, wq, kvn, wkv, ch, sh, *lo_tables)


EXP_ROWS = 16
PIPE_SLOTS = 4


def _attn_kernel(qt_ref, k_ref, vt_ref, o_ref, *scratch):
    ns = PIPE_SLOTS
    s_refs, cmax_refs, p_refs, alpha_refs = (scratch[j * ns:(j + 1) * ns] for j in range(4))
    m_ref, acc_ref = scratch[4 * ns:]
    tq = qt_ref.shape[1]
    tk = ATTN_KEY_CHUNK
    i = pl.program_id(1)
    n = (i * tq) // tk

    def chunk_start(c):
        return pl.multiple_of(jnp.clip(c, 0, n) * tk, tk)

    def stage_a(c, slot):
        s = _dot(k_ref[pl.ds(chunk_start(c), tk), :], qt_ref[...])
        s_refs[slot][...] = s
        cmax_refs[slot][...] = jnp.max(s, axis=0, keepdims=True)

    def stage_b(slot, causal_chunk=None):
        s_ref = s_refs[slot]
        if causal_chunk is None:
            cmax = cmax_refs[slot][...]
        else:
            kpos = causal_chunk * tk + lax.broadcasted_iota(jnp.int32, s_ref.shape, 0)
            qpos = i * tq + lax.broadcasted_iota(jnp.int32, s_ref.shape, 1)
            s = jnp.where(kpos <= qpos, s_ref[...], NEG_INF)
            s_ref[...] = s
            cmax = jnp.max(s, axis=0, keepdims=True)
        m_prev = m_ref[...]
        m_new = jnp.maximum(m_prev, cmax)
        alpha_refs[slot][...] = jnp.exp2(m_prev - m_new)
        m_ref[...] = m_new
        for r in range(0, tk, EXP_ROWS):
            p_refs[slot][r:r + EXP_ROWS, :] = jnp.exp2(s_ref[r:r + EXP_ROWS, :] - m_new).astype(BF16)

    def stage_c(c, slot):
        pv = _dot(vt_ref[:, pl.ds(chunk_start(c), tk)], p_refs[slot][...])
        acc_ref[...] = alpha_refs[slot][...] * acc_ref[...] + pv

    def group(u, x, y, with_c=True):
        if with_c:
            stage_c(2 * u - 2, y[0])
            stage_c(2 * u - 1, y[1])
        stage_b(x[0])
        stage_b(x[1])
        stage_a(2 * u + 2, y[0])
        stage_a(2 * u + 3, y[1])

    def finish(x, y, n_is_odd, with_c=True):
        if with_c:
            stage_c(n - n % 2 - 2, y[0])
            stage_c(n - n % 2 - 1, y[1])
        if n_is_odd:
            stage_b(x[0])
            stage_b(x[1], causal_chunk=n)
            stage_c(n - 1, x[0])
            stage_c(n, x[1])
        else:
            stage_b(x[0], causal_chunk=n)
            stage_c(n, x[0])
        acc = acc_ref[...]
        o_ref[...] = (acc[:LANES] / acc[LANES:LANES + 1]).T.astype(o_ref.dtype)

    lo, hi = (0, 1), (2, 3)
    m_ref[...] = jnp.full(m_ref.shape, -jnp.inf, F32)
    acc_ref[...] = jnp.zeros_like(acc_ref)
    stage_a(0, lo[0])
    stage_a(1, lo[1])

    @pl.when(n >= 2)
    def _():
        group(0, lo, hi, with_c=False)

    def body(u, carry):
        @pl.when(u % 2 == 0)
        def _():
            group(u, lo, hi)

        @pl.when(u % 2 == 1)
        def _():
            group(u, hi, lo)

        return carry

    lax.fori_loop(1, n // 2, body, 0)
    for r in range(2):
        @pl.when(n == r)
        def _(r=r):
            finish(lo, hi, n_is_odd=bool(r), with_c=False)

    for r in range(4):
        @pl.when((n >= 2) & (n % 4 == r))
        def _(r=r):
            x, y = (lo, hi) if r < 2 else (hi, lo)
            finish(x, y, n_is_odd=bool(r % 2))


def _attention(qt, k, vt, *, name):
    S = k.shape[0]
    tq = ATTN_Q_TILE
    tk = ATTN_KEY_CHUNK
    return pl.pallas_call(
        _attn_kernel,
        grid=(HEADS, S // tq),
        in_specs=[pl.BlockSpec((QK_WIDTH, tq), lambda h, i: (h, i)),
                  pl.BlockSpec((S, QK_WIDTH), lambda h, i: (0, h)),
                  pl.BlockSpec((V_ROWS, S), lambda h, i: (h, 0))],
        out_specs=pl.BlockSpec((tq, LANES), lambda h, i: (i, h)),
        out_shape=jax.ShapeDtypeStruct((S, LANES * HEADS), BF16),
        scratch_shapes=[pltpu.VMEM((tk, tq), F32)] * PIPE_SLOTS
        + [pltpu.VMEM((1, tq), F32)] * PIPE_SLOTS
        + [pltpu.VMEM((tk, tq), BF16)] * PIPE_SLOTS
        + [pltpu.VMEM((1, tq), F32)] * PIPE_SLOTS
        + [pltpu.VMEM((1, tq), F32),
           pltpu.VMEM((V_ROWS, tq), F32)],
        compiler_params=_params(2),
        name=name,
    )(qt, k, vt)


def _layer_norm(z, g, b):
    mu = jnp.mean(z, axis=-1, keepdims=True)
    d = z - mu
    var = jnp.mean(d * d, axis=-1, keepdims=True)
    return d * lax.rsqrt(var + LN_EPS) * g + b


def _post_kernel(x_ref, a_ref, wo_ref, win_ref, wout_ref, g1_ref, b1_ref, g2_ref, b2_ref, o_ref):
    h = _dot(a_ref[...], wo_ref[...])
    x1 = _layer_norm(DEEPNORM_ALPHA * x_ref[...] + h, g1_ref[...], b1_ref[...])
    x1b = x1.astype(BF16)
    y = None
    off = 0
    for c in FF_CHUNKS:
        gate = _dot(x1b, win_ref[:, off:off + c])
        up = _dot(x1b, win_ref[:, D_FF + off:D_FF + off + c])
        act = (gate * jax.nn.sigmoid(gate) * up).astype(BF16)
        part = _dot(act, wout_ref[off:off + c, :])
        y = part if y is None else y + part
        off += c
    o_ref[...] = _layer_norm(DEEPNORM_ALPHA * x1 + y, g2_ref[...], b2_ref[...])


def _post(x, a, wo, win, wout, g1, b1, g2, b2):
    S = x.shape[0]
    tm = FFN_ROW_TILE
    row = pl.BlockSpec((tm, D_MODEL), lambda i: (i, 0))
    full = lambda arr: _resident(arr.shape, lambda i: (0, 0))
    return pl.pallas_call(
        _post_kernel,
        grid=(S // tm,),
        in_specs=[row, row, full(wo), full(win), full(wout), full(g1), full(b1), full(g2), full(b2)],
        out_specs=row,
        out_shape=jax.ShapeDtypeStruct((S, D_MODEL), F32),
        compiler_params=_params(1),
        name="post",
    )(x, a, wo, win, wout, g1, b1, g2, b2)


def _moba_proj_kernel(x_ref, w_ref, ch_ref, sh_ref, pc_ref, qc_ref, pa_ref, qa_ref, pb_ref, qb_ref,
                      qt_ref, k_ref, vt_ref, kmean_ref):
    H = MOBA_HEADS
    tm = x_ref.shape[0]
    t = pl.program_id(0)
    qscale = (MOBA_HEAD_DIM ** -0.5) * LOG2E
    half = MOBA_ROT_DIM // 2

    @pl.when(t == 0)
    def _():
        kmean_ref[...] = jnp.zeros_like(kmean_ref)

    qkv = _dot(x_ref[...].astype(BF16), w_ref[...])
    cc = _angle_table(ch_ref, sh_ref, pc_ref, qc_ref)
    sa = _angle_table(ch_ref, sh_ref, pa_ref, qa_ref)
    sb = _angle_table(ch_ref, sh_ref, pb_ref, qb_ref)

    def rope(z):
        return z * cc + pltpu.roll(z, LANES - half, 1) * sa + pltpu.roll(z, half, 1) * sb

    blocks = tm // MOBA_BLOCK
    nb = kmean_ref.shape[0]
    nblk = lax.broadcasted_iota(jnp.int32, (nb, tm), 0).astype(F32)
    qpos = t * tm + lax.broadcasted_iota(jnp.int32, (nb, tm), 1)
    own = (qpos // MOBA_BLOCK).astype(F32)
    past = nblk < own
    kmean_row = lax.broadcasted_iota(jnp.int32, (nb, LANES), 0)
    key_blk = (t * tm + lax.broadcasted_iota(jnp.int32, (tm, LANES), 0)) // MOBA_BLOCK
    blk_onehot = (key_blk == lax.broadcasted_iota(jnp.int32, (tm, LANES), 1)).astype(BF16)
    never = jnp.full((LANES - nb, tm), NEG_INF, BF16)
    for h in range(H):
        c = LANES * h
        qh = rope(qkv[:, c:c + LANES])
        kh = rope(qkv[:, H * LANES + c:H * LANES + c + LANES])
        k_ref[:, 2 * c:2 * c + LANES] = kh.astype(BF16)
        k_ref[:, 2 * c + LANES:2 * c + QK_WIDTH] = blk_onehot
        _store_vt(vt_ref, h, qkv[:, 2 * H * LANES + c:2 * H * LANES + c + LANES])
        km = kmean_ref[:, c:c + LANES]
        for b in range(blocks):
            mean_b = jnp.mean(kh[b * MOBA_BLOCK:(b + 1) * MOBA_BLOCK], axis=0, keepdims=True)
            km = jnp.where(kmean_row == t * blocks + b, mean_b, km)
        kmean_ref[:, c:c + LANES] = km
        km_hi = km.astype(BF16)
        km_lo = (km - km_hi.astype(F32)).astype(BF16)
        q_hi = qh.astype(BF16)
        q_lo = (qh - q_hi.astype(F32)).astype(BF16)
        g = _dot_nt(km_hi, q_hi) + (_dot_nt(km_hi, q_lo) + _dot_nt(km_lo, q_hi))
        g = jnp.where(past, g, NEG_INF)
        for _ in range(MOBA_TOPK):
            best = jnp.max(g, axis=0, keepdims=True)
            idx = jnp.min(jnp.where(g == best, nblk, float(nb)), axis=0, keepdims=True)
            g = jnp.where(nblk == idx, -jnp.inf, g)
        sel = ((g == -jnp.inf) & past) | (nblk == own)
        qt_ref[2 * c:2 * c + LANES, :] = (qh * qscale).T.astype(BF16)
        qt_ref[2 * c + LANES:2 * c + LANES + nb, :] = jnp.where(sel, 0.0, NEG_INF).astype(BF16)
        qt_ref[2 * c + LANES + nb:2 * c + QK_WIDTH, :] = never


def _moba_proj(x, w, ch, sh, lo_tables):
    S = x.shape[0]
    tm = PROJ_ROW_TILE
    row = lambda wd: pl.BlockSpec((tm, wd), lambda i: (i, 0))
    return pl.pallas_call(
        _moba_proj_kernel,
        grid=(S // tm,),
        in_specs=[row(D_MODEL), _resident(w.shape, lambda i: (0, 0)), _hi_spec(tm), _hi_spec(tm)]
        + [_resident(t.shape, lambda i: (0, 0)) for t in lo_tables],
        out_specs=_qkv_out_specs(tm),
        out_shape=_qkv_out_shapes(S),
        scratch_shapes=[pltpu.VMEM((S // MOBA_BLOCK, LANES * MOBA_HEADS), F32)],
        compiler_params=_params(1),
        name="moba_proj",
    )(x, w, ch, sh, *lo_tables)


def _rotary_factors(seq_len, tm, lane_freq, combos):
    lo = jnp.arange(LANES, dtype=F32)[:, None] * lane_freq[None, :]
    hi = (jnp.arange(seq_len // LANES, dtype=F32) * LANES)[:, None] * lane_freq[None, :]
    cl, sl = jnp.cos(lo), jnp.sin(lo)
    shape = (seq_len // tm, tm // LANES, LANES)
    lo_tables = []
    for a, b in combos:
        lo_tables += [a[None, :] * cl + b[None, :] * sl, b[None, :] * cl - a[None, :] * sl]
    return jnp.cos(hi).reshape(shape), jnp.sin(hi).reshape(shape), lo_tables


def _swap_halves(w, width):
    shp = w.shape
    w = w.reshape(shp[0], -1, 2, width // 2)
    return w[:, :, ::-1, :].reshape(shp)


def kernel(x, mla_w_dqkv, mla_q_norm, mla_w_uq, mla_kv_norm, mla_w_ukv, mla_w_o, moba_w_qkv, moba_w_o,
           ffn_w_in, ffn_w_out, ln_mix_g, ln_mix_b, ln_ffn_g, ln_ffn_b):
    B, S, D = x.shape
    assert B == 1 and D == D_MODEL and S % ATTN_KEY_CHUNK == 0 and S // MOBA_BLOCK <= LANES and (S // MOBA_BLOCK) % BF16_SUBLANES == 0
    assert MLA_HEADS == HEADS and MOBA_HEADS == HEADS and ATTN_KEY_CHUNK % ATTN_Q_TILE == 0
    xs = x[0]
    H = MLA_HEADS

    lane = jnp.arange(LANES)
    zero, one = jnp.zeros((LANES,), F32), jnp.ones((LANES,), F32)
    f_mla = ROPE_THETA ** (-jnp.arange(0, MLA_ROPE_DIM, 2, dtype=F32) / MLA_ROPE_DIM)
    hr = MLA_ROPE_DIM // 2
    sign = jnp.where(lane % MLA_ROPE_DIM < hr, -1.0, 1.0).astype(F32)
    mla_tables = _rotary_factors(S, PROJ_ROW_TILE, f_mla[lane % hr], [(one, zero), (zero, sign)])
    f_moba = ROPE_THETA ** (-jnp.arange(0, MOBA_ROT_DIM, 2, dtype=F32) / MOBA_ROT_DIM)
    hr = MOBA_ROT_DIM // 2
    rot = lane < MOBA_ROT_DIM
    first = (lane < hr).astype(F32)
    second = (rot & (lane >= hr)).astype(F32)
    moba_tables = _rotary_factors(S, PROJ_ROW_TILE, jnp.where(rot, f_moba[lane % hr], 0.0),
                                  [(one, zero), (zero, -first), (zero, second)])

    wd = mla_w_dqkv[0]
    w_kr = wd[:, MLA_Q_RANK + MLA_KV_RANK:]
    w_kr_sw = _swap_halves(w_kr, MLA_ROPE_DIM)
    wd_ext = jnp.concatenate([wd[:, :MLA_Q_RANK + MLA_KV_RANK], w_kr, w_kr, w_kr_sw, w_kr_sw], axis=1).astype(BF16)
    wq = mla_w_uq[0].reshape(MLA_Q_RANK, H, MLA_QK_DIM)
    wq_nope = wq[:, :, :MLA_NOPE_DIM].reshape(MLA_Q_RANK, H * MLA_NOPE_DIM)
    wq_rope = wq[:, :, MLA_NOPE_DIM:].reshape(MLA_Q_RANK, H * MLA_ROPE_DIM)
    wq_ext = jnp.concatenate([wq_nope, wq_rope, _swap_halves(wq_rope, MLA_ROPE_DIM)], axis=1).astype(BF16)

    qt, k, vt = _mla_proj(xs, wd_ext, mla_q_norm[0][None, :], wq_ext, mla_kv_norm[0][None, :],
                          mla_w_ukv[0].astype(BF16), *mla_tables)
    a = _attention(qt, k, vt, name="mla_attn")
    x1 = _post(xs, a, mla_w_o[0].astype(BF16), ffn_w_in[0].astype(BF16), ffn_w_out[0].astype(BF16),
               ln_mix_g[0][None, :], ln_mix_b[0][None, :], ln_ffn_g[0][None, :], ln_ffn_b[0][None, :])

    qt, k, vt = _moba_proj(x1, moba_w_qkv[0].astype(BF16), *moba_tables)
    a = _attention(qt, k, vt, name="moba_attn")
    x2 = _post(x1, a, moba_w_o[0].astype(BF16), ffn_w_in[1].astype(BF16), ffn_w_out[1].astype(BF16),
               ln_mix_g[1][None, :], ln_mix_b[1][None, :], ln_ffn_g[1][None, :], ln_ffn_b[1][None, :])
    return x2[None]
```

```python
import math

import jax
import jax.numpy as jnp
from jax import lax
from jax.experimental import pallas as pl
from jax.experimental.pallas import tpu as pltpu

D_MODEL = 1024
DEPTH = 2
ROPE_THETA = 500000.0
NEG_INF = -1e30
LN_EPS = 1e-5
RMS_EPS = 1e-6
MLA_HEADS = 8
MLA_Q_RANK = 384
MLA_KV_RANK = 256
MLA_NOPE_DIM = 128
MLA_ROPE_DIM = 64
MLA_V_DIM = 128
MLA_QK_DIM = MLA_NOPE_DIM + MLA_ROPE_DIM
MOBA_HEADS = 8
MOBA_HEAD_DIM = D_MODEL // MOBA_HEADS
MOBA_ROT_DIM = MOBA_HEAD_DIM // 4
MOBA_BLOCK = 256
MOBA_TOPK = 3
D_FF = 2816
DEEPNORM_ALPHA = (2 * DEPTH) ** 0.25

LANES = 128
BF16_SUBLANES = 16
LOG2E = math.log2(math.e)

HEADS = 8
QK_WIDTH = 2 * LANES
V_ROWS = LANES + BF16_SUBLANES
PROJ_ROW_TILE = 512
FFN_ROW_TILE = 512
ATTN_Q_TILE = 1024
ATTN_KEY_CHUNK = 1024
FF_CHUNKS = (768, 768, 768, 512)
VMEM_LIMIT = 56 * 1024 * 1024

BF16 = jnp.bfloat16
F32 = jnp.float32


def _dot(a, b, precision=None):
    return jnp.dot(a, b, preferred_element_type=F32, precision=precision)


def _dot_nt(a, b, precision=None):
    return lax.dot_general(a, b, (((1,), (1,)), ((), ())), preferred_element_type=F32, precision=precision)


def _resident(shape, index_map):
    return pl.BlockSpec(shape, index_map, pipeline_mode=pl.Buffered(1))


def _params(n_grid_dims):
    return pltpu.CompilerParams(dimension_semantics=("arbitrary",) * n_grid_dims, vmem_limit_bytes=VMEM_LIMIT)


def _qkv_out_specs(tm):
    return [pl.BlockSpec((HEADS * QK_WIDTH, tm), lambda i: (0, i)),
            pl.BlockSpec((tm, HEADS * QK_WIDTH), lambda i: (i, 0)),
            pl.BlockSpec((HEADS * V_ROWS, tm), lambda i: (0, i))]


def _qkv_out_shapes(S):
    return [jax.ShapeDtypeStruct((HEADS * QK_WIDTH, S), BF16),
            jax.ShapeDtypeStruct((S, HEADS * QK_WIDTH), BF16),
            jax.ShapeDtypeStruct((HEADS * V_ROWS, S), BF16)]


def _store_vt(vt_ref, h, v):
    tm = v.shape[0]
    r = V_ROWS * h
    vt_ref[r:r + LANES, :] = v.T.astype(BF16)
    pad_row = lax.broadcasted_iota(jnp.int32, (BF16_SUBLANES, tm), 0)
    vt_ref[r + LANES:r + V_ROWS, :] = (pad_row == 0).astype(BF16)


def _rms(x, g):
    return x * lax.rsqrt(jnp.mean(x * x, axis=-1, keepdims=True) + RMS_EPS) * g


def _angle_table(ch_ref, sh_ref, p_ref, q_ref):
    p = p_ref[...]
    q = q_ref[...]
    return jnp.concatenate([ch_ref[0, g:g + 1, :] * p + sh_ref[0, g:g + 1, :] * q
                            for g in range(ch_ref.shape[1])], axis=0)


def _mla_proj_kernel(x_ref, wd_ref, qn_ref, wq_ref, kvn_ref, wkv_ref, ch_ref, sh_ref, pc_ref, qc_ref, ps_ref, qs_ref,
                     qt_ref, k_ref, vt_ref):
    H = MLA_HEADS
    qscale = (MLA_QK_DIM ** -0.5) * LOG2E
    xb = x_ref[...].astype(BF16)
    lat = _dot(xb, wd_ref[...])
    o1 = MLA_Q_RANK
    o2 = o1 + MLA_KV_RANK
    cq = _rms(lat[:, :o1], qn_ref[...]).astype(BF16)
    ckv = _rms(lat[:, o1:o2], kvn_ref[...]).astype(BF16)
    tc = _angle_table(ch_ref, sh_ref, pc_ref, qc_ref)
    ts = _angle_table(ch_ref, sh_ref, ps_ref, qs_ref)
    kr = lat[:, o2:o2 + LANES] * tc + lat[:, o2 + LANES:o2 + 2 * LANES] * ts
    lane = lax.broadcasted_iota(jnp.int32, kr.shape, 1)
    kr_lo = jnp.where(lane < MLA_ROPE_DIM, kr, 0.0).astype(BF16)
    kr_hi = jnp.where(lane >= MLA_ROPE_DIM, kr, 0.0).astype(BF16)
    q = _dot(cq, wq_ref[...])
    kv = _dot(ckv, wkv_ref[...])
    r0 = H * MLA_NOPE_DIM
    r1 = r0 + H * MLA_ROPE_DIM
    for p in range(H // 2):
        qr = (q[:, r0 + LANES * p:r0 + LANES * (p + 1)] * tc
              + q[:, r1 + LANES * p:r1 + LANES * (p + 1)] * ts)
        qr_t = (qr * qscale).T.astype(BF16)
        for h in (2 * p, 2 * p + 1):
            c = QK_WIDTH * h
            qt_ref[c:c + LANES, :] = (q[:, LANES * h:LANES * (h + 1)] * qscale).T.astype(BF16)
            qt_ref[c + LANES:c + QK_WIDTH, :] = qr_t
            k_ref[:, c:c + LANES] = kv[:, c:c + LANES].astype(BF16)
            k_ref[:, c + LANES:c + QK_WIDTH] = kr_lo if h % 2 == 0 else kr_hi
            _store_vt(vt_ref, h, kv[:, c + LANES:c + QK_WIDTH])


def _hi_spec(tm):
    return pl.BlockSpec((1, tm // LANES, LANES), lambda i: (i, 0, 0))


def _mla_proj(x, wd, qn, wq, kvn, wkv, ch, sh, lo_tables):
    S = x.shape[0]
    tm = PROJ_ROW_TILE
    row = lambda w: pl.BlockSpec((tm, w), lambda i: (i, 0))
    full = lambda a: _resident(a.shape, lambda i: (0, 0))
    return pl.pallas_call(
        _mla_proj_kernel,
        grid=(S // tm,),
        in_specs=[row(D_MODEL), full(wd), full(qn), full(wq), full(kvn), full(wkv), _hi_spec(tm), _hi_spec(tm)]
        + [full(t) for t in lo_tables],
        out_specs=_qkv_out_specs(tm),
        out_shape=_qkv_out_shapes(S),
        compiler_params=_params(1),
        name="mla_proj",
    )(x, wd, qn, wq, kvn, wkv, ch, sh, *lo_tables)


EXP_ROWS = 16
PIPE_SLOTS = 4


def _attn_kernel(qt_ref, k_ref, vt_ref, o_ref, *scratch):
    ns = PIPE_SLOTS
    s_refs, cmax_refs, p_refs, alpha_refs = (scratch[j * ns:(j + 1) * ns] for j in range(4))
    m_ref, acc_ref = scratch[4 * ns:]
    tq = qt_ref.shape[1]
    tk = ATTN_KEY_CHUNK
    i = pl.program_id(1)
    n = (i * tq) // tk

    def chunk_start(c):
        return pl.multiple_of(jnp.clip(c, 0, n) * tk, tk)

    def stage_a(c, slot):
        s = _dot(k_ref[pl.ds(chunk_start(c), tk), :], qt_ref[...])
        s_refs[slot][...] = s
        cmax_refs[slot][...] = jnp.max(s, axis=0, keepdims=True)

    def stage_b(slot, causal_chunk=None):
        s_ref = s_refs[slot]
        if causal_chunk is None:
            cmax = cmax_refs[slot][...]
        else:
            kpos = causal_chunk * tk + lax.broadcasted_iota(jnp.int32, s_ref.shape, 0)
            qpos = i * tq + lax.broadcasted_iota(jnp.int32, s_ref.shape, 1)
            s = jnp.where(kpos <= qpos, s_ref[...], NEG_INF)
            s_ref[...] = s
            cmax = jnp.max(s, axis=0, keepdims=True)
        m_prev = m_ref[...]
        m_new = jnp.maximum(m_prev, cmax)
        alpha_refs[slot][...] = jnp.exp2(m_prev - m_new)
        m_ref[...] = m_new
        for r in range(0, tk, EXP_ROWS):
            p_refs[slot][r:r + EXP_ROWS, :] = jnp.exp2(s_ref[r:r + EXP_ROWS, :] - m_new).astype(BF16)

    def stage_c(c, slot):
        pv = _dot(vt_ref[:, pl.ds(chunk_start(c), tk)], p_refs[slot][...])
        acc_ref[...] = alpha_refs[slot][...] * acc_ref[...] + pv

    def group(u, x, y, with_c=True):
        if with_c:
            stage_c(2 * u - 2, y[0])
            stage_c(2 * u - 1, y[1])
        stage_b(x[0])
        stage_b(x[1])
        stage_a(2 * u + 2, y[0])
        stage_a(2 * u + 3, y[1])

    def finish(x, y, n_is_odd, with_c=True):
        if with_c:
            stage_c(n - n % 2 - 2, y[0])
            stage_c(n - n % 2 - 1, y[1])
        if n_is_odd:
            stage_b(x[0])
            stage_b(x[1], causal_chunk=n)
            stage_c(n - 1, x[0])
            stage_c(n, x[1])
        else:
            stage_b(x[0], causal_chunk=n)
            stage_c(n, x[0])
        acc = acc_ref[...]
        o_ref[...] = (acc[:LANES] / acc[LANES:LANES + 1]).T.astype(o_ref.dtype)

    lo, hi = (0, 1), (2, 3)
    m_ref[...] = jnp.full(m_ref.shape, -jnp.inf, F32)
    acc_ref[...] = jnp.zeros_like(acc_ref)
    stage_a(0, lo[0])
    stage_a(1, lo[1])

    @pl.when(n >= 2)
    def _():
        group(0, lo, hi, with_c=False)

    def body(u, carry):
        @pl.when(u % 2 == 0)
        def _():
            group(u, lo, hi)

        @pl.when(u % 2 == 1)
        def _():
            group(u, hi, lo)

        return carry

    lax.fori_loop(1, n // 2, body, 0)
    for r in range(2):
        @pl.when(n == r)
        def _(r=r):
            finish(lo, hi, n_is_odd=bool(r), with_c=False)

    for r in range(4):
        @pl.when((n >= 2) & (n % 4 == r))
        def _(r=r):
            x, y = (lo, hi) if r < 2 else (hi, lo)
            finish(x, y, n_is_odd=bool(r % 2))


def _attention(qt, k, vt, *, name):
    S = k.shape[0]
    tq = ATTN_Q_TILE
    tk = ATTN_KEY_CHUNK
    return pl.pallas_call(
        _attn_kernel,
        grid=(HEADS, S // tq),
        in_specs=[pl.BlockSpec((QK_WIDTH, tq), lambda h, i: (h, i)),
                  pl.BlockSpec((S, QK_WIDTH), lambda h, i: (0, h)),
                  pl.BlockSpec((V_ROWS, S), lambda h, i: (h, 0))],
        out_specs=pl.BlockSpec((tq, LANES), lambda h, i: (i, h)),
        out_shape=jax.ShapeDtypeStruct((S, LANES * HEADS), BF16),
        scratch_shapes=[pltpu.VMEM((tk, tq), F32)] * PIPE_SLOTS
        + [pltpu.VMEM((1, tq), F32)] * PIPE_SLOTS
        + [pltpu.VMEM((tk, tq), BF16)] * PIPE_SLOTS
        + [pltpu.VMEM((1, tq), F32)] * PIPE_SLOTS
        + [pltpu.VMEM((1, tq), F32),
           pltpu.VMEM((V_ROWS, tq), F32)],
        compiler_params=_params(2),
        name=name,
    )(qt, k, vt)


def _layer_norm(z, g, b):
    mu = jnp.mean(z, axis=-1, keepdims=True)
    d = z - mu
    var = jnp.mean(d * d, axis=-1, keepdims=True)
    return d * lax.rsqrt(var + LN_EPS) * g + b


def _post_kernel(x_ref, a_ref, wo_ref, win_ref, wout_ref, g1_ref, b1_ref, g2_ref, b2_ref, o_ref):
    h = _dot(a_ref[...], wo_ref[...])
    x1 = _layer_norm(DEEPNORM_ALPHA * x_ref[...] + h, g1_ref[...], b1_ref[...])
    x1b = x1.astype(BF16)
    y = None
    off = 0
    for c in FF_CHUNKS:
        gate = _dot(x1b, win_ref[:, off:off + c])
        up = _dot(x1b, win_ref[:, D_FF + off:D_FF + off + c])
        act = (gate * jax.nn.sigmoid(gate) * up).astype(BF16)
        part = _dot(act, wout_ref[off:off + c, :])
        y = part if y is None else y + part
        off += c
    o_ref[...] = _layer_norm(DEEPNORM_ALPHA * x1 + y, g2_ref[...], b2_ref[...])


def _post(x, a, wo, win, wout, g1, b1, g2, b2, *, mixer_layer, layer):
    S = x.shape[0]
    tm = FFN_ROW_TILE
    row = pl.BlockSpec((tm, D_MODEL), lambda i: (i, 0))

    def layer_of(arr, l):
        return _resident((None,) + arr.shape[1:], lambda i: (l,) + (0,) * (arr.ndim - 1))

    return pl.pallas_call(
        _post_kernel,
        grid=(S // tm,),
        in_specs=[row, row, layer_of(wo, mixer_layer)] + [layer_of(t, layer) for t in (win, wout, g1, b1, g2, b2)],
        out_specs=row,
        out_shape=jax.ShapeDtypeStruct((S, D_MODEL), F32),
        compiler_params=_params(1),
        name="post",
    )(x, a, wo, win, wout, g1, b1, g2, b2)


def _moba_proj_kernel(x_ref, w_ref, ch_ref, sh_ref, pc_ref, qc_ref, pa_ref, qa_ref, pb_ref, qb_ref,
                      qt_ref, k_ref, vt_ref, kmean_ref):
    H = MOBA_HEADS
    tm = x_ref.shape[0]
    t = pl.program_id(0)
    qscale = (MOBA_HEAD_DIM ** -0.5) * LOG2E
    half = MOBA_ROT_DIM // 2

    @pl.when(t == 0)
    def _():
        kmean_ref[...] = jnp.zeros_like(kmean_ref)

    qkv = _dot(x_ref[...].astype(BF16), w_ref[...])
    cc = _angle_table(ch_ref, sh_ref, pc_ref, qc_ref)
    sa = _angle_table(ch_ref, sh_ref, pa_ref, qa_ref)
    sb = _angle_table(ch_ref, sh_ref, pb_ref, qb_ref)

    def rope(z):
        return z * cc + pltpu.roll(z, LANES - half, 1) * sa + pltpu.roll(z, half, 1) * sb

    blocks = tm // MOBA_BLOCK
    nb = kmean_ref.shape[0]
    nblk = lax.broadcasted_iota(jnp.int32, (nb, tm), 0).astype(F32)
    qpos = t * tm + lax.broadcasted_iota(jnp.int32, (nb, tm), 1)
    own = (qpos // MOBA_BLOCK).astype(F32)
    past = nblk < own
    kmean_row = lax.broadcasted_iota(jnp.int32, (nb, LANES), 0)
    key_blk = (t * tm + lax.broadcasted_iota(jnp.int32, (tm, LANES), 0)) // MOBA_BLOCK
    blk_onehot = (key_blk == lax.broadcasted_iota(jnp.int32, (tm, LANES), 1)).astype(BF16)
    never = jnp.full((LANES - nb, tm), NEG_INF, BF16)
    for h in range(H):
        c = LANES * h
        qh = rope(qkv[:, c:c + LANES])
        kh = rope(qkv[:, H * LANES + c:H * LANES + c + LANES])
        k_ref[:, 2 * c:2 * c + LANES] = kh.astype(BF16)
        k_ref[:, 2 * c + LANES:2 * c + QK_WIDTH] = blk_onehot
        _store_vt(vt_ref, h, qkv[:, 2 * H * LANES + c:2 * H * LANES + c + LANES])
        km = kmean_ref[:, c:c + LANES]
        for b in range(blocks):
            mean_b = jnp.mean(kh[b * MOBA_BLOCK:(b + 1) * MOBA_BLOCK], axis=0, keepdims=True)
            km = jnp.where(kmean_row == t * blocks + b, mean_b, km)
        kmean_ref[:, c:c + LANES] = km
        km_hi = km.astype(BF16)
        km_lo = (km - km_hi.astype(F32)).astype(BF16)
        q_hi = qh.astype(BF16)
        q_lo = (qh - q_hi.astype(F32)).astype(BF16)
        g = _dot_nt(km_hi, q_hi) + (_dot_nt(km_hi, q_lo) + _dot_nt(km_lo, q_hi))
        g = jnp.where(past, g, NEG_INF)
        for _ in range(MOBA_TOPK):
            best = jnp.max(g, axis=0, keepdims=True)
            idx = jnp.min(jnp.where(g == best, nblk, float(nb)), axis=0, keepdims=True)
            g = jnp.where(nblk == idx, -jnp.inf, g)
        sel = ((g == -jnp.inf) & past) | (nblk == own)
        qt_ref[2 * c:2 * c + LANES, :] = (qh * qscale).T.astype(BF16)
        qt_ref[2 * c + LANES:2 * c + LANES + nb, :] = jnp.where(sel, 0.0, NEG_INF).astype(BF16)
        qt_ref[2 * c + LANES + nb:2 * c + QK_WIDTH, :] = never


def _moba_proj(x, w, ch, sh, lo_tables):
    S = x.shape[0]
    tm = PROJ_ROW_TILE
    row = lambda wd: pl.BlockSpec((tm, wd), lambda i: (i, 0))
    return pl.pallas_call(
        _moba_proj_kernel,
        grid=(S // tm,),
        in_specs=[row(D_MODEL), _resident(w.shape, lambda i: (0, 0)), _hi_spec(tm), _hi_spec(tm)]
        + [_resident(t.shape, lambda i: (0, 0)) for t in lo_tables],
        out_specs=_qkv_out_specs(tm),
        out_shape=_qkv_out_shapes(S),
        scratch_shapes=[pltpu.VMEM((S // MOBA_BLOCK, LANES * MOBA_HEADS), F32)],
        compiler_params=_params(1),
        name="moba_proj",
    )(x, w, ch, sh, *lo_tables)


def _rotary_factors(seq_len, tm, lane_freq, combos):
    lo = jnp.arange(LANES, dtype=F32)[:, None] * lane_freq[None, :]
    hi = (jnp.arange(seq_len // LANES, dtype=F32) * LANES)[:, None] * lane_freq[None, :]
    cl, sl = jnp.cos(lo), jnp.sin(lo)
    shape = (seq_len // tm, tm // LANES, LANES)
    lo_tables = []
    for a, b in combos:
        lo_tables += [a[None, :] * cl + b[None, :] * sl, b[None, :] * cl - a[None, :] * sl]
    return jnp.cos(hi).reshape(shape), jnp.sin(hi).reshape(shape), lo_tables


def _swap_halves(w, width):
    shp = w.shape
    w = w.reshape(shp[0], -1, 2, width // 2)
    return w[:, :, ::-1, :].reshape(shp)


def kernel(x, mla_w_dqkv, mla_q_norm, mla_w_uq, mla_kv_norm, mla_w_ukv, mla_w_o, moba_w_qkv, moba_w_o,
           ffn_w_in, ffn_w_out, ln_mix_g, ln_mix_b, ln_ffn_g, ln_ffn_b):
    B, S, D = x.shape
    assert B == 1 and D == D_MODEL and S % ATTN_KEY_CHUNK == 0 and S // MOBA_BLOCK <= LANES and (S // MOBA_BLOCK) % BF16_SUBLANES == 0
    assert MLA_HEADS == HEADS and MOBA_HEADS == HEADS and ATTN_KEY_CHUNK % ATTN_Q_TILE == 0
    xs = x[0]
    H = MLA_HEADS

    lane = jnp.arange(LANES)
    zero, one = jnp.zeros((LANES,), F32), jnp.ones((LANES,), F32)
    f_mla = ROPE_THETA ** (-jnp.arange(0, MLA_ROPE_DIM, 2, dtype=F32) / MLA_ROPE_DIM)
    hr = MLA_ROPE_DIM // 2
    sign = jnp.where(lane % MLA_ROPE_DIM < hr, -1.0, 1.0).astype(F32)
    mla_tables = _rotary_factors(S, PROJ_ROW_TILE, f_mla[lane % hr], [(one, zero), (zero, sign)])
    f_moba = ROPE_THETA ** (-jnp.arange(0, MOBA_ROT_DIM, 2, dtype=F32) / MOBA_ROT_DIM)
    hr = MOBA_ROT_DIM // 2
    rot = lane < MOBA_ROT_DIM
    first = (lane < hr).astype(F32)
    second = (rot & (lane >= hr)).astype(F32)
    moba_tables = _rotary_factors(S, PROJ_ROW_TILE, jnp.where(rot, f_moba[lane % hr], 0.0),
                                  [(one, zero), (zero, -first), (zero, second)])

    wd = mla_w_dqkv[0]
    w_kr = wd[:, MLA_Q_RANK + MLA_KV_RANK:]
    w_kr_sw = _swap_halves(w_kr, MLA_ROPE_DIM)
    wd_ext = jnp.concatenate([wd[:, :MLA_Q_RANK + MLA_KV_RANK], w_kr, w_kr, w_kr_sw, w_kr_sw], axis=1).astype(BF16)
    wq = mla_w_uq[0].reshape(MLA_Q_RANK, H, MLA_QK_DIM)
    wq_nope = wq[:, :, :MLA_NOPE_DIM].reshape(MLA_Q_RANK, H * MLA_NOPE_DIM)
    wq_rope = wq[:, :, MLA_NOPE_DIM:].reshape(MLA_Q_RANK, H * MLA_ROPE_DIM)
    wq_ext = jnp.concatenate([wq_nope, wq_rope, _swap_halves(wq_rope, MLA_ROPE_DIM)], axis=1).astype(BF16)

    qt, k, vt = _mla_proj(xs, wd_ext, mla_q_norm[0][None, :], wq_ext, mla_kv_norm[0][None, :],
                          mla_w_ukv[0].astype(BF16), *mla_tables)
    w_in, w_out = ffn_w_in.astype(BF16), ffn_w_out.astype(BF16)
    norms = [p[:, None, :] for p in (ln_mix_g, ln_mix_b, ln_ffn_g, ln_ffn_b)]

    a = _attention(qt, k, vt, name="mla_attn")
    x1 = _post(xs, a, mla_w_o.astype(BF16), w_in, w_out, *norms, mixer_layer=0, layer=0)

    qt, k, vt = _moba_proj(x1, moba_w_qkv[0].astype(BF16), *moba_tables)
    a = _attention(qt, k, vt, name="moba_attn")
    x2 = _post(x1, a, moba_w_o.astype(BF16), w_in, w_out, *norms, mixer_layer=0, layer=1)
    return x2[None]
```

```python
import math

import jax
import jax.numpy as jnp
from jax import lax
from jax.experimental import pallas as pl
from jax.experimental.pallas import tpu as pltpu

D_MODEL = 1024
DEPTH = 2
ROPE_THETA = 500000.0
NEG_INF = -1e30
LN_EPS = 1e-5
RMS_EPS = 1e-6
MLA_HEADS = 8
MLA_Q_RANK = 384
MLA_KV_RANK = 256
MLA_NOPE_DIM = 128
MLA_ROPE_DIM = 64
MLA_QK_DIM = MLA_NOPE_DIM + MLA_ROPE_DIM
MOBA_HEADS = 8
MOBA_HEAD_DIM = D_MODEL // MOBA_HEADS
MOBA_ROT_DIM = MOBA_HEAD_DIM // 4
MOBA_BLOCK = 256
MOBA_TOPK = 3
D_FF = 2816
DEEPNORM_ALPHA = (2 * DEPTH) ** 0.25

LANES = 128
BF16_SUBLANES = 16
LOG2E = math.log2(math.e)

HEADS = 8
QK_WIDTH = 2 * LANES
V_ROWS = LANES + BF16_SUBLANES
PROJ_ROW_TILE = 512
FFN_ROW_TILE = 512
ATTN_Q_TILE = 1024
ATTN_KEY_CHUNK = 1024
FF_CHUNKS = (768, 768, 768, 512)
VMEM_LIMIT = 56 * 1024 * 1024

BF16 = jnp.bfloat16
F32 = jnp.float32


def _dot(a, b):
    return jnp.dot(a, b, preferred_element_type=F32)


def _dot_nt(a, b):
    return lax.dot_general(a, b, (((1,), (1,)), ((), ())), preferred_element_type=F32)


def _resident(shape, index_map):
    return pl.BlockSpec(shape, index_map, pipeline_mode=pl.Buffered(1))


def _params(n_grid_dims):
    return pltpu.CompilerParams(dimension_semantics=("arbitrary",) * n_grid_dims, vmem_limit_bytes=VMEM_LIMIT)


def _qkv_out_specs(tm):
    return [pl.BlockSpec((HEADS * QK_WIDTH, tm), lambda i: (0, i)),
            pl.BlockSpec((tm, HEADS * QK_WIDTH), lambda i: (i, 0)),
            pl.BlockSpec((HEADS * V_ROWS, tm), lambda i: (0, i))]


def _qkv_out_shapes(S):
    return [jax.ShapeDtypeStruct((HEADS * QK_WIDTH, S), BF16),
            jax.ShapeDtypeStruct((S, HEADS * QK_WIDTH), BF16),
            jax.ShapeDtypeStruct((HEADS * V_ROWS, S), BF16)]


def _store_vt(vt_ref, h, v):
    tm = v.shape[0]
    r = V_ROWS * h
    vt_ref[r:r + LANES, :] = v.T.astype(BF16)
    pad_row = lax.broadcasted_iota(jnp.int32, (BF16_SUBLANES, tm), 0)
    vt_ref[r + LANES:r + V_ROWS, :] = (pad_row == 0).astype(BF16)


def _rms(x, g):
    return x * lax.rsqrt(jnp.mean(x * x, axis=-1, keepdims=True) + RMS_EPS) * g


def _angle_table(ch_ref, sh_ref, p_ref, q_ref):
    p = p_ref[...]
    q = q_ref[...]
    return jnp.concatenate([ch_ref[0, g:g + 1, :] * p + sh_ref[0, g:g + 1, :] * q
                            for g in range(ch_ref.shape[1])], axis=0)


def _mla_proj_kernel(x_ref, wd_ref, qn_ref, wq_ref, kvn_ref, wkv_ref, ch_ref, sh_ref, pc_ref, qc_ref, ps_ref, qs_ref,
                     qt_ref, k_ref, vt_ref):
    H = MLA_HEADS
    qscale = (MLA_QK_DIM ** -0.5) * LOG2E
    xb = x_ref[...].astype(BF16)
    lat = _dot(xb, wd_ref[...])
    o1 = MLA_Q_RANK
    o2 = o1 + MLA_KV_RANK
    cq = _rms(lat[:, :o1], qn_ref[...]).astype(BF16)
    ckv = _rms(lat[:, o1:o2], kvn_ref[...]).astype(BF16)
    tc = _angle_table(ch_ref, sh_ref, pc_ref, qc_ref)
    ts = _angle_table(ch_ref, sh_ref, ps_ref, qs_ref)
    kr = lat[:, o2:o2 + LANES] * tc + lat[:, o2 + LANES:o2 + 2 * LANES] * ts
    lane = lax.broadcasted_iota(jnp.int32, kr.shape, 1)
    kr_lo = jnp.where(lane < MLA_ROPE_DIM, kr, 0.0).astype(BF16)
    kr_hi = jnp.where(lane >= MLA_ROPE_DIM, kr, 0.0).astype(BF16)
    q = _dot(cq, wq_ref[...])
    kv = _dot(ckv, wkv_ref[...])
    r0 = H * MLA_NOPE_DIM
    r1 = r0 + H * MLA_ROPE_DIM
    for p in range(H // 2):
        qr = (q[:, r0 + LANES * p:r0 + LANES * (p + 1)] * tc
              + q[:, r1 + LANES * p:r1 + LANES * (p + 1)] * ts)
        qr_t = (qr * qscale).T.astype(BF16)
        for h in (2 * p, 2 * p + 1):
            c = QK_WIDTH * h
            qt_ref[c:c + LANES, :] = (q[:, LANES * h:LANES * (h + 1)] * qscale).T.astype(BF16)
            qt_ref[c + LANES:c + QK_WIDTH, :] = qr_t
            k_ref[:, c:c + LANES] = kv[:, c:c + LANES].astype(BF16)
            k_ref[:, c + LANES:c + QK_WIDTH] = kr_lo if h % 2 == 0 else kr_hi
            _store_vt(vt_ref, h, kv[:, c + LANES:c + QK_WIDTH])


def _hi_spec(tm):
    return pl.BlockSpec((1, tm // LANES, LANES), lambda i: (i, 0, 0))


def _mla_proj(x, wd, qn, wq, kvn, wkv, ch, sh, lo_tables):
    S = x.shape[0]
    tm = PROJ_ROW_TILE
    row = lambda w: pl.BlockSpec((tm, w), lambda i: (i, 0))
    full = lambda a: _resident(a.shape, lambda i: (0, 0))
    return pl.pallas_call(
        _mla_proj_kernel,
        grid=(S // tm,),
        in_specs=[row(D_MODEL), full(wd), full(qn), full(wq), full(kvn), full(wkv), _hi_spec(tm), _hi_spec(tm)]
        + [full(t) for t in lo_tables],
        out_specs=_qkv_out_specs(tm),
        out_shape=_qkv_out_shapes(S),
        compiler_params=_params(1),
        name="mla_proj",
    )(x, wd, qn, wq, kvn, wkv, ch, sh, *lo_tables)


EXP_ROWS = 16
PIPE_SLOTS = 4


def _attn_kernel(qt_ref, k_ref, vt_ref, o_ref, *scratch):
    ns = PIPE_SLOTS
    s_refs, cmax_refs, p_refs, alpha_refs = (scratch[j * ns:(j + 1) * ns] for j in range(4))
    m_ref, acc_ref = scratch[4 * ns:]
    tq = qt_ref.shape[1]
    tk = ATTN_KEY_CHUNK
    i = pl.program_id(1)
    n = (i * tq) // tk

    def chunk_start(c):
        return pl.multiple_of(jnp.clip(c, 0, n) * tk, tk)

    def stage_a(c, slot):
        s = _dot(k_ref[pl.ds(chunk_start(c), tk), :], qt_ref[...])
        s_refs[slot][...] = s
        cmax_refs[slot][...] = jnp.max(s, axis=0, keepdims=True)

    def stage_b(slot, causal_chunk=None):
        s_ref = s_refs[slot]
        if causal_chunk is None:
            cmax = cmax_refs[slot][...]
        else:
            kpos = causal_chunk * tk + lax.broadcasted_iota(jnp.int32, s_ref.shape, 0)
            qpos = i * tq + lax.broadcasted_iota(jnp.int32, s_ref.shape, 1)
            s = jnp.where(kpos <= qpos, s_ref[...], NEG_INF)
            s_ref[...] = s
            cmax = jnp.max(s, axis=0, keepdims=True)
        m_prev = m_ref[...]
        m_new = jnp.maximum(m_prev, cmax)
        alpha_refs[slot][...] = jnp.exp2(m_prev - m_new)
        m_ref[...] = m_new
        for r in range(0, tk, EXP_ROWS):
            p_refs[slot][r:r + EXP_ROWS, :] = jnp.exp2(s_ref[r:r + EXP_ROWS, :] - m_new).astype(BF16)

    def stage_c(c, slot):
        pv = _dot(vt_ref[:, pl.ds(chunk_start(c), tk)], p_refs[slot][...])
        acc_ref[...] = alpha_refs[slot][...] * acc_ref[...] + pv

    def group(u, x, y, with_c=True):
        if with_c:
            stage_c(2 * u - 2, y[0])
            stage_c(2 * u - 1, y[1])
        stage_b(x[0])
        stage_b(x[1])
        stage_a(2 * u + 2, y[0])
        stage_a(2 * u + 3, y[1])

    def finish(x, y, n_is_odd, with_c=True):
        if with_c:
            stage_c(n - n % 2 - 2, y[0])
            stage_c(n - n % 2 - 1, y[1])
        if n_is_odd:
            stage_b(x[0])
            stage_b(x[1], causal_chunk=n)
            stage_c(n - 1, x[0])
            stage_c(n, x[1])
        else:
            stage_b(x[0], causal_chunk=n)
            stage_c(n, x[0])
        acc = acc_ref[...]
        o_ref[...] = (acc[:LANES] / acc[LANES:LANES + 1]).T.astype(o_ref.dtype)

    lo, hi = (0, 1), (2, 3)
    m_ref[...] = jnp.full(m_ref.shape, -jnp.inf, F32)
    acc_ref[...] = jnp.zeros_like(acc_ref)
    stage_a(0, lo[0])
    stage_a(1, lo[1])

    @pl.when(n >= 2)
    def _():
        group(0, lo, hi, with_c=False)

    def body(u, carry):
        @pl.when(u % 2 == 0)
        def _():
            group(u, lo, hi)

        @pl.when(u % 2 == 1)
        def _():
            group(u, hi, lo)

        return carry

    lax.fori_loop(1, n // 2, body, 0)
    for r in range(2):
        @pl.when(n == r)
        def _(r=r):
            finish(lo, hi, n_is_odd=bool(r), with_c=False)

    for r in range(4):
        @pl.when((n >= 2) & (n % 4 == r))
        def _(r=r):
            x, y = (lo, hi) if r < 2 else (hi, lo)
            finish(x, y, n_is_odd=bool(r % 2))


def _attention(qt, k, vt, *, name):
    S = k.shape[0]
    tq = ATTN_Q_TILE
    tk = ATTN_KEY_CHUNK
    return pl.pallas_call(
        _attn_kernel,
        grid=(HEADS, S // tq),
        in_specs=[pl.BlockSpec((QK_WIDTH, tq), lambda h, i: (h, i)),
                  pl.BlockSpec((S, QK_WIDTH), lambda h, i: (0, h)),
                  pl.BlockSpec((V_ROWS, S), lambda h, i: (h, 0))],
        out_specs=pl.BlockSpec((tq, LANES), lambda h, i: (i, h)),
        out_shape=jax.ShapeDtypeStruct((S, LANES * HEADS), BF16),
        scratch_shapes=[pltpu.VMEM((tk, tq), F32)] * PIPE_SLOTS
        + [pltpu.VMEM((1, tq), F32)] * PIPE_SLOTS
        + [pltpu.VMEM((tk, tq), BF16)] * PIPE_SLOTS
        + [pltpu.VMEM((1, tq), F32)] * PIPE_SLOTS
        + [pltpu.VMEM((1, tq), F32),
           pltpu.VMEM((V_ROWS, tq), F32)],
        compiler_params=_params(2),
        name=name,
    )(qt, k, vt)


def _layer_norm(z, g, b):
    mu = jnp.mean(z, axis=-1, keepdims=True)
    d = z - mu
    var = jnp.mean(d * d, axis=-1, keepdims=True)
    return d * lax.rsqrt(var + LN_EPS) * g + b


def _post_kernel(x_ref, a_ref, wo_ref, win_ref, wout_ref, g1_ref, b1_ref, g2_ref, b2_ref, o_ref):
    h = _dot(a_ref[...], wo_ref[...])
    x1 = _layer_norm(DEEPNORM_ALPHA * x_ref[...] + h, g1_ref[...], b1_ref[...])
    x1b = x1.astype(BF16)
    y = None
    off = 0
    for c in FF_CHUNKS:
        gate = _dot(x1b, win_ref[:, off:off + c])
        up = _dot(x1b, win_ref[:, D_FF + off:D_FF + off + c])
        act = (gate * jax.nn.sigmoid(gate) * up).astype(BF16)
        part = _dot(act, wout_ref[off:off + c, :])
        y = part if y is None else y + part
        off += c
    o_ref[...] = _layer_norm(DEEPNORM_ALPHA * x1 + y, g2_ref[...], b2_ref[...])


def _post(x, a, wo, win, wout, g1, b1, g2, b2, *, mixer_layer, layer):
    S = x.shape[0]
    tm = FFN_ROW_TILE
    row = pl.BlockSpec((tm, D_MODEL), lambda i: (i, 0))

    def layer_of(arr, l):
        return _resident((None,) + arr.shape[1:], lambda i: (l,) + (0,) * (arr.ndim - 1))

    return pl.pallas_call(
        _post_kernel,
        grid=(S // tm,),
        in_specs=[row, row, layer_of(wo, mixer_layer)] + [layer_of(t, layer) for t in (win, wout, g1, b1, g2, b2)],
        out_specs=row,
        out_shape=jax.ShapeDtypeStruct((S, D_MODEL), F32),
        compiler_params=_params(1),
        name="post",
    )(x, a, wo, win, wout, g1, b1, g2, b2)


def _moba_proj_kernel(x_ref, w_ref, ch_ref, sh_ref, pc_ref, qc_ref, pa_ref, qa_ref, pb_ref, qb_ref,
                      qt_ref, k_ref, vt_ref, kmean_ref):
    H = MOBA_HEADS
    tm = x_ref.shape[0]
    t = pl.program_id(0)
    qscale = (MOBA_HEAD_DIM ** -0.5) * LOG2E
    half = MOBA_ROT_DIM // 2

    @pl.when(t == 0)
    def _():
        kmean_ref[...] = jnp.zeros_like(kmean_ref)

    qkv = _dot(x_ref[...].astype(BF16), w_ref[...])
    cc = _angle_table(ch_ref, sh_ref, pc_ref, qc_ref)
    sa = _angle_table(ch_ref, sh_ref, pa_ref, qa_ref)
    sb = _angle_table(ch_ref, sh_ref, pb_ref, qb_ref)

    def rope(z):
        return z * cc + pltpu.roll(z, LANES - half, 1) * sa + pltpu.roll(z, half, 1) * sb

    blocks = tm // MOBA_BLOCK
    nb = kmean_ref.shape[0]
    nblk = lax.broadcasted_iota(jnp.int32, (nb, tm), 0).astype(F32)
    qpos = t * tm + lax.broadcasted_iota(jnp.int32, (nb, tm), 1)
    own = (qpos // MOBA_BLOCK).astype(F32)
    past = nblk < own
    kmean_row = lax.broadcasted_iota(jnp.int32, (nb, LANES), 0)
    key_blk = (t * tm + lax.broadcasted_iota(jnp.int32, (tm, LANES), 0)) // MOBA_BLOCK
    blk_onehot = (key_blk == lax.broadcasted_iota(jnp.int32, (tm, LANES), 1)).astype(BF16)
    never = jnp.full((LANES - nb, tm), NEG_INF, BF16)
    for h in range(H):
        c = LANES * h
        qh = rope(qkv[:, c:c + LANES])
        kh = rope(qkv[:, H * LANES + c:H * LANES + c + LANES])
        k_ref[:, 2 * c:2 * c + LANES] = kh.astype(BF16)
        k_ref[:, 2 * c + LANES:2 * c + QK_WIDTH] = blk_onehot
        _store_vt(vt_ref, h, qkv[:, 2 * H * LANES + c:2 * H * LANES + c + LANES])
        km = kmean_ref[:, c:c + LANES]
        for b in range(blocks):
            mean_b = jnp.mean(kh[b * MOBA_BLOCK:(b + 1) * MOBA_BLOCK], axis=0, keepdims=True)
            km = jnp.where(kmean_row == t * blocks + b, mean_b, km)
        kmean_ref[:, c:c + LANES] = km
        km_hi = km.astype(BF16)
        km_lo = (km - km_hi.astype(F32)).astype(BF16)
        q_hi = qh.astype(BF16)
        q_lo = (qh - q_hi.astype(F32)).astype(BF16)
        g = _dot_nt(km_hi, q_hi) + (_dot_nt(km_hi, q_lo) + _dot_nt(km_lo, q_hi))
        g = jnp.where(past, g, NEG_INF)
        for _ in range(MOBA_TOPK):
            best = jnp.max(g, axis=0, keepdims=True)
            idx = jnp.min(jnp.where(g == best, nblk, float(nb)), axis=0, keepdims=True)
            g = jnp.where(nblk == idx, -jnp.inf, g)
        sel = ((g == -jnp.inf) & past) | (nblk == own)
        qt_ref[2 * c:2 * c + LANES, :] = (qh * qscale).T.astype(BF16)
        qt_ref[2 * c + LANES:2 * c + LANES + nb, :] = jnp.where(sel, 0.0, NEG_INF).astype(BF16)
        qt_ref[2 * c + LANES + nb:2 * c + QK_WIDTH, :] = never


def _moba_proj(x, w, ch, sh, lo_tables):
    S = x.shape[0]
    tm = PROJ_ROW_TILE
    row = lambda wd: pl.BlockSpec((tm, wd), lambda i: (i, 0))
    return pl.pallas_call(
        _moba_proj_kernel,
        grid=(S // tm,),
        in_specs=[row(D_MODEL), _resident(w.shape, lambda i: (0, 0)), _hi_spec(tm), _hi_spec(tm)]
        + [_resident(t.shape, lambda i: (0, 0)) for t in lo_tables],
        out_specs=_qkv_out_specs(tm),
        out_shape=_qkv_out_shapes(S),
        scratch_shapes=[pltpu.VMEM((S // MOBA_BLOCK, LANES * MOBA_HEADS), F32)],
        compiler_params=_params(1),
        name="moba_proj",
    )(x, w, ch, sh, *lo_tables)


def _rotary_factors(seq_len, tm, lane_freq, combos):
    lo = jnp.arange(LANES, dtype=F32)[:, None] * lane_freq[None, :]
    hi = (jnp.arange(seq_len // LANES, dtype=F32) * LANES)[:, None] * lane_freq[None, :]
    cl, sl = jnp.cos(lo), jnp.sin(lo)
    shape = (seq_len // tm, tm // LANES, LANES)
    lo_tables = []
    for a, b in combos:
        lo_tables += [a[None, :] * cl + b[None, :] * sl, b[None, :] * cl - a[None, :] * sl]
    return jnp.cos(hi).reshape(shape), jnp.sin(hi).reshape(shape), lo_tables


def _swap_halves(w, width):
    shp = w.shape
    w = w.reshape(shp[0], -1, 2, width // 2)
    return w[:, :, ::-1, :].reshape(shp)


def kernel(x, mla_w_dqkv, mla_q_norm, mla_w_uq, mla_kv_norm, mla_w_ukv, mla_w_o, moba_w_qkv, moba_w_o,
           ffn_w_in, ffn_w_out, ln_mix_g, ln_mix_b, ln_ffn_g, ln_ffn_b):
    B, S, D = x.shape
    assert B == 1 and D == D_MODEL and S % ATTN_KEY_CHUNK == 0 and S // MOBA_BLOCK <= LANES and (S // MOBA_BLOCK) % BF16_SUBLANES == 0
    assert MLA_HEADS == HEADS and MOBA_HEADS == HEADS and ATTN_KEY_CHUNK % ATTN_Q_TILE == 0
    xs = x[0]
    H = MLA_HEADS

    lane = jnp.arange(LANES)
    zero, one = jnp.zeros((LANES,), F32), jnp.ones((LANES,), F32)
    f_mla = ROPE_THETA ** (-jnp.arange(0, MLA_ROPE_DIM, 2, dtype=F32) / MLA_ROPE_DIM)
    hr = MLA_ROPE_DIM // 2
    sign = jnp.where(lane % MLA_ROPE_DIM < hr, -1.0, 1.0).astype(F32)
    mla_tables = _rotary_factors(S, PROJ_ROW_TILE, jnp.tile(f_mla, LANES // hr), [(one, zero), (zero, sign)])
    f_moba = ROPE_THETA ** (-jnp.arange(0, MOBA_ROT_DIM, 2, dtype=F32) / MOBA_ROT_DIM)
    hr = MOBA_ROT_DIM // 2
    rot = lane < MOBA_ROT_DIM
    first = (lane < hr).astype(F32)
    second = (rot & (lane >= hr)).astype(F32)
    moba_tables = _rotary_factors(S, PROJ_ROW_TILE, jnp.where(rot, jnp.tile(f_moba, LANES // hr), 0.0),
                                  [(one, zero), (zero, -first), (zero, second)])

    wd = mla_w_dqkv[0]
    w_kr = wd[:, MLA_Q_RANK + MLA_KV_RANK:]
    w_kr_sw = _swap_halves(w_kr, MLA_ROPE_DIM)
    wd_ext = jnp.concatenate([wd[:, :MLA_Q_RANK + MLA_KV_RANK], w_kr, w_kr, w_kr_sw, w_kr_sw], axis=1).astype(BF16)
    wq = mla_w_uq[0].reshape(MLA_Q_RANK, H, MLA_QK_DIM)
    wq_nope = wq[:, :, :MLA_NOPE_DIM].reshape(MLA_Q_RANK, H * MLA_NOPE_DIM)
    wq_rope = wq[:, :, MLA_NOPE_DIM:].reshape(MLA_Q_RANK, H * MLA_ROPE_DIM)
    wq_ext = jnp.concatenate([wq_nope, wq_rope, _swap_halves(wq_rope, MLA_ROPE_DIM)], axis=1).astype(BF16)

    qt, k, vt = _mla_proj(xs, wd_ext, mla_q_norm[0][None, :], wq_ext, mla_kv_norm[0][None, :],
                          mla_w_ukv[0].astype(BF16), *mla_tables)
    w_in, w_out = ffn_w_in.astype(BF16), ffn_w_out.astype(BF16)
    norms = [p[:, None, :] for p in (ln_mix_g, ln_mix_b, ln_ffn_g, ln_ffn_b)]

    a = _attention(qt, k, vt, name="mla_attn")
    x1 = _post(xs, a, mla_w_o.astype(BF16), w_in, w_out, *norms, mixer_layer=0, layer=0)

    qt, k, vt = _moba_proj(x1, moba_w_qkv[0].astype(BF16), *moba_tables)
    a = _attention(qt, k, vt, name="moba_attn")
    x2 = _post(x1, a, moba_w_o.astype(BF16), w_in, w_out, *norms, mixer_layer=0, layer=1)
    return x2[None]
```

```python
import math

import jax
import jax.numpy as jnp
from jax import lax
from jax.experimental import pallas as pl
from jax.experimental.pallas import tpu as pltpu

D_MODEL = 1024
DEPTH = 2
ROPE_THETA = 500000.0
NEG_INF = -1e30
LN_EPS = 1e-5
RMS_EPS = 1e-6
MLA_HEADS = 8
MLA_Q_RANK = 384
MLA_KV_RANK = 256
MLA_NOPE_DIM = 128
MLA_ROPE_DIM = 64
MLA_QK_DIM = MLA_NOPE_DIM + MLA_ROPE_DIM
MOBA_HEADS = 8
MOBA_HEAD_DIM = D_MODEL // MOBA_HEADS
MOBA_ROT_DIM = MOBA_HEAD_DIM // 4
MOBA_BLOCK = 256
MOBA_TOPK = 3
D_FF = 2816
DEEPNORM_ALPHA = (2 * DEPTH) ** 0.25

LANES = 128
BF16_SUBLANES = 16
LOG2E = math.log2(math.e)

HEADS = 8
QK_WIDTH = 2 * LANES
V_ROWS = LANES + BF16_SUBLANES
PROJ_ROW_TILE = 512
FFN_ROW_TILE = 512
ATTN_Q_TILE = 1024
ATTN_KEY_CHUNK = 1024
FF_CHUNKS = (768, 768, 768, 512)
VMEM_LIMIT = 56 * 1024 * 1024

BF16 = jnp.bfloat16
F32 = jnp.float32


def _dot(a, b):
    return jnp.dot(a, b, preferred_element_type=F32)


def _dot_nt(a, b):
    return lax.dot_general(a, b, (((1,), (1,)), ((), ())), preferred_element_type=F32)


def _resident(shape, index_map):
    return pl.BlockSpec(shape, index_map, pipeline_mode=pl.Buffered(1))


def _params(n_grid_dims):
    return pltpu.CompilerParams(dimension_semantics=("arbitrary",) * n_grid_dims, vmem_limit_bytes=VMEM_LIMIT)


def _qkv_out_specs(tm):
    return [pl.BlockSpec((HEADS * QK_WIDTH, tm), lambda i: (0, i)),
            pl.BlockSpec((tm, HEADS * QK_WIDTH), lambda i: (i, 0)),
            pl.BlockSpec((HEADS * V_ROWS, tm), lambda i: (0, i))]


def _qkv_out_shapes(S):
    return [jax.ShapeDtypeStruct((HEADS * QK_WIDTH, S), BF16),
            jax.ShapeDtypeStruct((S, HEADS * QK_WIDTH), BF16),
            jax.ShapeDtypeStruct((HEADS * V_ROWS, S), BF16)]


def _store_vt(vt_ref, h, v):
    tm = v.shape[0]
    r = V_ROWS * h
    vt_ref[r:r + LANES, :] = v.T.astype(BF16)
    pad_row = lax.broadcasted_iota(jnp.int32, (BF16_SUBLANES, tm), 0)
    vt_ref[r + LANES:r + V_ROWS, :] = (pad_row == 0).astype(BF16)


def _rms(x, g):
    return x * lax.rsqrt(jnp.mean(x * x, axis=-1, keepdims=True) + RMS_EPS) * g


def _angle_table(ch_ref, sh_ref, p_ref, q_ref):
    p = p_ref[...]
    q = q_ref[...]
    return jnp.concatenate([ch_ref[0, g:g + 1, :] * p + sh_ref[0, g:g + 1, :] * q
                            for g in range(ch_ref.shape[1])], axis=0)


def _mla_proj_kernel(x_ref, wd_ref, qn_ref, wq_ref, kvn_ref, wkv_ref, ch_ref, sh_ref, pc_ref, qc_ref, ps_ref, qs_ref,
                     qt_ref, k_ref, vt_ref):
    H = MLA_HEADS
    qscale = (MLA_QK_DIM ** -0.5) * LOG2E
    xb = x_ref[...].astype(BF16)
    lat = _dot(xb, wd_ref[...])
    o1 = MLA_Q_RANK
    o2 = o1 + MLA_KV_RANK
    cq = _rms(lat[:, :o1], qn_ref[...]).astype(BF16)
    ckv = _rms(lat[:, o1:o2], kvn_ref[...]).astype(BF16)
    tc = _angle_table(ch_ref, sh_ref, pc_ref, qc_ref)
    ts = _angle_table(ch_ref, sh_ref, ps_ref, qs_ref)
    kr = lat[:, o2:o2 + LANES] * tc + lat[:, o2 + LANES:o2 + 2 * LANES] * ts
    lane = lax.broadcasted_iota(jnp.int32, kr.shape, 1)
    kr_lo = jnp.where(lane < MLA_ROPE_DIM, kr, 0.0).astype(BF16)
    kr_hi = jnp.where(lane >= MLA_ROPE_DIM, kr, 0.0).astype(BF16)
    q = _dot(cq, wq_ref[...])
    kv = _dot(ckv, wkv_ref[...])
    r0 = H * MLA_NOPE_DIM
    r1 = r0 + H * MLA_ROPE_DIM
    for p in range(H // 2):
        qr = (q[:, r0 + LANES * p:r0 + LANES * (p + 1)] * tc
              + q[:, r1 + LANES * p:r1 + LANES * (p + 1)] * ts)
        qr_t = (qr * qscale).T.astype(BF16)
        for h in (2 * p, 2 * p + 1):
            c = QK_WIDTH * h
            qt_ref[c:c + LANES, :] = (q[:, LANES * h:LANES * (h + 1)] * qscale).T.astype(BF16)
            qt_ref[c + LANES:c + QK_WIDTH, :] = qr_t
            k_ref[:, c:c + LANES] = kv[:, c:c + LANES].astype(BF16)
            k_ref[:, c + LANES:c + QK_WIDTH] = kr_lo if h % 2 == 0 else kr_hi
            _store_vt(vt_ref, h, kv[:, c + LANES:c + QK_WIDTH])


def _hi_spec(tm):
    return pl.BlockSpec((1, tm // LANES, LANES), lambda i: (i, 0, 0))


def _mla_proj(x, wd, qn, wq, kvn, wkv, ch, sh, lo_tables):
    S = x.shape[0]
    tm = PROJ_ROW_TILE
    row = lambda w: pl.BlockSpec((tm, w), lambda i: (i, 0))
    full = lambda a: _resident(a.shape, lambda i: (0, 0))
    return pl.pallas_call(
        _mla_proj_kernel,
        grid=(S // tm,),
        in_specs=[row(D_MODEL), full(wd), full(qn), full(wq), full(kvn), full(wkv), _hi_spec(tm), _hi_spec(tm)]
        + [full(t) for t in lo_tables],
        out_specs=_qkv_out_specs(tm),
        out_shape=_qkv_out_shapes(S),
        compiler_params=_params(1),
        name="mla_proj",
    )(x, wd, qn, wq, kvn, wkv, ch, sh, *lo_tables)


EXP_ROWS = 16
PIPE_SLOTS = 4


def _attn_kernel(qt_ref, k_ref, vt_ref, o_ref, *scratch):
    ns = PIPE_SLOTS
    s_refs, cmax_refs, p_refs, alpha_refs = (scratch[j * ns:(j + 1) * ns] for j in range(4))
    m_ref, acc_ref = scratch[4 * ns:]
    tq = qt_ref.shape[1]
    tk = ATTN_KEY_CHUNK
    i = pl.program_id(1)
    n = (i * tq) // tk

    def chunk_start(c):
        return pl.multiple_of(jnp.clip(c, 0, n) * tk, tk)

    def stage_a(c, slot):
        s = _dot(k_ref[pl.ds(chunk_start(c), tk), :], qt_ref[...])
        s_refs[slot][...] = s
        cmax_refs[slot][...] = jnp.max(s, axis=0, keepdims=True)

    def stage_b(slot, causal_chunk=None):
        s_ref = s_refs[slot]
        if causal_chunk is None:
            cmax = cmax_refs[slot][...]
        else:
            kpos = lax.broadcasted_iota(jnp.int32, s_ref.shape, 0)
            qpos = lax.broadcasted_iota(jnp.int32, s_ref.shape, 1)
            if tq != tk:
                qpos = qpos + (i * tq - causal_chunk * tk)
            s = jnp.where(kpos <= qpos, s_ref[...], NEG_INF)
            s_ref[...] = s
            cmax = jnp.max(s, axis=0, keepdims=True)
        m_prev = m_ref[...]
        m_new = jnp.maximum(m_prev, cmax)
        alpha_refs[slot][...] = jnp.exp2(m_prev - m_new)
        m_ref[...] = m_new
        for r in range(0, tk, EXP_ROWS):
            p_refs[slot][r:r + EXP_ROWS, :] = jnp.exp2(s_ref[r:r + EXP_ROWS, :] - m_new).astype(BF16)

    def stage_c(c, slot):
        pv = _dot(vt_ref[:, pl.ds(chunk_start(c), tk)], p_refs[slot][...])
        acc_ref[...] = alpha_refs[slot][...] * acc_ref[...] + pv

    def group(u, x, y, with_c=True):
        if with_c:
            stage_c(2 * u - 2, y[0])
            stage_c(2 * u - 1, y[1])
        stage_b(x[0])
        stage_b(x[1])
        stage_a(2 * u + 2, y[0])
        stage_a(2 * u + 3, y[1])

    def finish(x, y, n_is_odd, with_c=True):
        if with_c:
            stage_c(n - n % 2 - 2, y[0])
            stage_c(n - n % 2 - 1, y[1])
        if n_is_odd:
            stage_b(x[0])
            stage_b(x[1], causal_chunk=n)
            stage_c(n - 1, x[0])
            stage_c(n, x[1])
        else:
            stage_b(x[0], causal_chunk=n)
            stage_c(n, x[0])
        acc = acc_ref[...]
        o_ref[...] = (acc[:LANES] / acc[LANES:LANES + 1]).T.astype(o_ref.dtype)

    lo, hi = (0, 1), (2, 3)
    m_ref[...] = jnp.full(m_ref.shape, -jnp.inf, F32)
    acc_ref[...] = jnp.zeros_like(acc_ref)
    stage_a(0, lo[0])
    stage_a(1, lo[1])

    @pl.when(n >= 2)
    def _():
        group(0, lo, hi, with_c=False)

    def body(u, carry):
        @pl.when(u % 2 == 0)
        def _():
            group(u, lo, hi)

        @pl.when(u % 2 == 1)
        def _():
            group(u, hi, lo)

        return carry

    lax.fori_loop(1, n // 2, body, 0)
    for r in range(2):
        @pl.when(n == r)
        def _(r=r):
            finish(lo, hi, n_is_odd=bool(r), with_c=False)

    for r in range(4):
        @pl.when((n >= 2) & (n % 4 == r))
        def _(r=r):
            x, y = (lo, hi) if r < 2 else (hi, lo)
            finish(x, y, n_is_odd=bool(r % 2))


def _attention(qt, k, vt, *, name):
    S = k.shape[0]
    tq = ATTN_Q_TILE
    tk = ATTN_KEY_CHUNK
    return pl.pallas_call(
        _attn_kernel,
        grid=(HEADS, S // tq),
        in_specs=[pl.BlockSpec((QK_WIDTH, tq), lambda h, i: (h, i)),
                  pl.BlockSpec((S, QK_WIDTH), lambda h, i: (0, h)),
                  pl.BlockSpec((V_ROWS, S), lambda h, i: (h, 0))],
        out_specs=pl.BlockSpec((tq, LANES), lambda h, i: (i, h)),
        out_shape=jax.ShapeDtypeStruct((S, LANES * HEADS), BF16),
        scratch_shapes=[pltpu.VMEM((tk, tq), F32)] * PIPE_SLOTS
        + [pltpu.VMEM((1, tq), F32)] * PIPE_SLOTS
        + [pltpu.VMEM((tk, tq), BF16)] * PIPE_SLOTS
        + [pltpu.VMEM((1, tq), F32)] * PIPE_SLOTS
        + [pltpu.VMEM((1, tq), F32),
           pltpu.VMEM((V_ROWS, tq), F32)],
        compiler_params=_params(2),
        name=name,
    )(qt, k, vt)


def _layer_norm(z, g, b):
    mu = jnp.mean(z, axis=-1, keepdims=True)
    d = z - mu
    var = jnp.mean(d * d, axis=-1, keepdims=True)
    return d * lax.rsqrt(var + LN_EPS) * g + b


def _post_kernel(x_ref, a_ref, wo_ref, win_ref, wout_ref, g1_ref, b1_ref, g2_ref, b2_ref, o_ref):
    h = _dot(a_ref[...], wo_ref[...])
    x1 = _layer_norm(DEEPNORM_ALPHA * x_ref[...] + h, g1_ref[...], b1_ref[...])
    x1b = x1.astype(BF16)
    y = None
    off = 0
    for c in FF_CHUNKS:
        gate = _dot(x1b, win_ref[:, off:off + c])
        up = _dot(x1b, win_ref[:, D_FF + off:D_FF + off + c])
        act = (gate * jax.nn.sigmoid(gate) * up).astype(BF16)
        part = _dot(act, wout_ref[off:off + c, :])
        y = part if y is None else y + part
        off += c
    o_ref[...] = _layer_norm(DEEPNORM_ALPHA * x1 + y, g2_ref[...], b2_ref[...])


def _post(x, a, wo, win, wout, g1, b1, g2, b2, *, mixer_layer, layer):
    S = x.shape[0]
    tm = FFN_ROW_TILE
    row = pl.BlockSpec((tm, D_MODEL), lambda i: (i, 0))

    def layer_of(arr, l):
        return _resident((None,) + arr.shape[1:], lambda i: (l,) + (0,) * (arr.ndim - 1))

    return pl.pallas_call(
        _post_kernel,
        grid=(S // tm,),
        in_specs=[row, row, layer_of(wo, mixer_layer)] + [layer_of(t, layer) for t in (win, wout, g1, b1, g2, b2)],
        out_specs=row,
        out_shape=jax.ShapeDtypeStruct((S, D_MODEL), F32),
        compiler_params=_params(1),
        name="post",
    )(x, a, wo, win, wout, g1, b1, g2, b2)


def _moba_proj_kernel(x_ref, w_ref, ch_ref, sh_ref, pc_ref, qc_ref, pa_ref, qa_ref, pb_ref, qb_ref,
                      qt_ref, k_ref, vt_ref, kmean_ref):
    H = MOBA_HEADS
    tm = x_ref.shape[0]
    t = pl.program_id(0)
    qscale = (MOBA_HEAD_DIM ** -0.5) * LOG2E
    half = MOBA_ROT_DIM // 2

    @pl.when(t == 0)
    def _():
        kmean_ref[...] = jnp.zeros_like(kmean_ref)

    qkv = _dot(x_ref[...].astype(BF16), w_ref[...])
    cc = _angle_table(ch_ref, sh_ref, pc_ref, qc_ref)
    sa = _angle_table(ch_ref, sh_ref, pa_ref, qa_ref)
    sb = _angle_table(ch_ref, sh_ref, pb_ref, qb_ref)

    def rope(z):
        return z * cc + pltpu.roll(z, LANES - half, 1) * sa + pltpu.roll(z, half, 1) * sb

    blocks = tm // MOBA_BLOCK
    nb = kmean_ref.shape[0]
    nblk = lax.broadcasted_iota(jnp.int32, (nb, tm), 0).astype(F32)
    qpos = t * tm + lax.broadcasted_iota(jnp.int32, (nb, tm), 1)
    own = (qpos // MOBA_BLOCK).astype(F32)
    past = nblk < own
    kmean_row = lax.broadcasted_iota(jnp.int32, (nb, LANES), 0)
    key_blk = (t * tm + lax.broadcasted_iota(jnp.int32, (tm, LANES), 0)) // MOBA_BLOCK
    blk_onehot = (key_blk == lax.broadcasted_iota(jnp.int32, (tm, LANES), 1)).astype(BF16)
    never = jnp.full((LANES - nb, tm), NEG_INF, BF16)
    for h in range(H):
        c = LANES * h
        qh = rope(qkv[:, c:c + LANES])
        kh = rope(qkv[:, H * LANES + c:H * LANES + c + LANES])
        k_ref[:, 2 * c:2 * c + LANES] = kh.astype(BF16)
        k_ref[:, 2 * c + LANES:2 * c + QK_WIDTH] = blk_onehot
        _store_vt(vt_ref, h, qkv[:, 2 * H * LANES + c:2 * H * LANES + c + LANES])
        km = kmean_ref[:, c:c + LANES]
        for b in range(blocks):
            mean_b = jnp.mean(kh[b * MOBA_BLOCK:(b + 1) * MOBA_BLOCK], axis=0, keepdims=True)
            km = jnp.where(kmean_row == t * blocks + b, mean_b, km)
        kmean_ref[:, c:c + LANES] = km
        km_hi = km.astype(BF16)
        km_lo = (km - km_hi.astype(F32)).astype(BF16)
        q_hi = qh.astype(BF16)
        q_lo = (qh - q_hi.astype(F32)).astype(BF16)
        g = _dot_nt(km_hi, q_hi) + (_dot_nt(km_hi, q_lo) + _dot_nt(km_lo, q_hi))
        g = jnp.where(past, g, NEG_INF)
        for _ in range(MOBA_TOPK):
            best = jnp.max(g, axis=0, keepdims=True)
            idx = jnp.min(jnp.where(g == best, nblk, float(nb)), axis=0, keepdims=True)
            g = jnp.where(nblk == idx, -jnp.inf, g)
        sel = ((g == -jnp.inf) & past) | (nblk == own)
        qt_ref[2 * c:2 * c + LANES, :] = (qh * qscale).T.astype(BF16)
        qt_ref[2 * c + LANES:2 * c + LANES + nb, :] = jnp.where(sel, 0.0, NEG_INF).astype(BF16)
        qt_ref[2 * c + LANES + nb:2 * c + QK_WIDTH, :] = never


def _moba_proj(x, w, ch, sh, lo_tables):
    S = x.shape[0]
    tm = PROJ_ROW_TILE
    row = lambda wd: pl.BlockSpec((tm, wd), lambda i: (i, 0))
    return pl.pallas_call(
        _moba_proj_kernel,
        grid=(S // tm,),
        in_specs=[row(D_MODEL), _resident(w.shape, lambda i: (0, 0)), _hi_spec(tm), _hi_spec(tm)]
        + [_resident(t.shape, lambda i: (0, 0)) for t in lo_tables],
        out_specs=_qkv_out_specs(tm),
        out_shape=_qkv_out_shapes(S),
        scratch_shapes=[pltpu.VMEM((S // MOBA_BLOCK, LANES * MOBA_HEADS), F32)],
        compiler_params=_params(1),
        name="moba_proj",
    )(x, w, ch, sh, *lo_tables)


def _rotary_factors(seq_len, tm, lane_freq, combos):
    lo = jnp.arange(LANES, dtype=F32)[:, None] * lane_freq[None, :]
    hi = (jnp.arange(seq_len // LANES, dtype=F32) * LANES)[:, None] * lane_freq[None, :]
    cl, sl = jnp.cos(lo), jnp.sin(lo)
    shape = (seq_len // tm, tm // LANES, LANES)
    lo_tables = []
    for a, b in combos:
        lo_tables += [a[None, :] * cl + b[None, :] * sl, b[None, :] * cl - a[None, :] * sl]
    return jnp.cos(hi).reshape(shape), jnp.sin(hi).reshape(shape), lo_tables


def _swap_halves(w, width):
    shp = w.shape
    w = w.reshape(shp[0], -1, 2, width // 2)
    return w[:, :, ::-1, :].reshape(shp)


def kernel(x, mla_w_dqkv, mla_q_norm, mla_w_uq, mla_kv_norm, mla_w_ukv, mla_w_o, moba_w_qkv, moba_w_o,
           ffn_w_in, ffn_w_out, ln_mix_g, ln_mix_b, ln_ffn_g, ln_ffn_b):
    B, S, D = x.shape
    assert B == 1 and D == D_MODEL and S % ATTN_KEY_CHUNK == 0 and S // MOBA_BLOCK <= LANES and (S // MOBA_BLOCK) % BF16_SUBLANES == 0
    assert MLA_HEADS == HEADS and MOBA_HEADS == HEADS and ATTN_KEY_CHUNK % ATTN_Q_TILE == 0
    xs = x[0]
    H = MLA_HEADS

    lane = jnp.arange(LANES)
    zero, one = jnp.zeros((LANES,), F32), jnp.ones((LANES,), F32)
    f_mla = ROPE_THETA ** (-jnp.arange(0, MLA_ROPE_DIM, 2, dtype=F32) / MLA_ROPE_DIM)
    hr = MLA_ROPE_DIM // 2
    sign = jnp.where(lane % MLA_ROPE_DIM < hr, -1.0, 1.0).astype(F32)
    mla_tables = _rotary_factors(S, PROJ_ROW_TILE, jnp.tile(f_mla, LANES // hr), [(one, zero), (zero, sign)])
    f_moba = ROPE_THETA ** (-jnp.arange(0, MOBA_ROT_DIM, 2, dtype=F32) / MOBA_ROT_DIM)
    hr = MOBA_ROT_DIM // 2
    rot = lane < MOBA_ROT_DIM
    first = (lane < hr).astype(F32)
    second = (rot & (lane >= hr)).astype(F32)
    moba_tables = _rotary_factors(S, PROJ_ROW_TILE, jnp.where(rot, jnp.tile(f_moba, LANES // hr), 0.0),
                                  [(one, zero), (zero, -first), (zero, second)])

    wd = mla_w_dqkv[0]
    w_kr = wd[:, MLA_Q_RANK + MLA_KV_RANK:]
    w_kr_sw = _swap_halves(w_kr, MLA_ROPE_DIM)
    wd_ext = jnp.concatenate([wd[:, :MLA_Q_RANK + MLA_KV_RANK], w_kr, w_kr, w_kr_sw, w_kr_sw], axis=1).astype(BF16)
    wq = mla_w_uq[0].reshape(MLA_Q_RANK, H, MLA_QK_DIM)
    wq_nope = wq[:, :, :MLA_NOPE_DIM].reshape(MLA_Q_RANK, H * MLA_NOPE_DIM)
    wq_rope = wq[:, :, MLA_NOPE_DIM:].reshape(MLA_Q_RANK, H * MLA_ROPE_DIM)
    wq_ext = jnp.concatenate([wq_nope, wq_rope, _swap_halves(wq_rope, MLA_ROPE_DIM)], axis=1).astype(BF16)

    qt, k, vt = _mla_proj(xs, wd_ext, mla_q_norm[0][None, :], wq_ext, mla_kv_norm[0][None, :],
                          mla_w_ukv[0].astype(BF16), *mla_tables)
    w_in, w_out = ffn_w_in.astype(BF16), ffn_w_out.astype(BF16)
    norms = [p[:, None, :] for p in (ln_mix_g, ln_mix_b, ln_ffn_g, ln_ffn_b)]

    a = _attention(qt, k, vt, name="mla_attn")
    x1 = _post(xs, a, mla_w_o.astype(BF16), w_in, w_out, *norms, mixer_layer=0, layer=0)

    qt, k, vt = _moba_proj(x1, moba_w_qkv[0].astype(BF16), *moba_tables)
    a = _attention(qt, k, vt, name="moba_attn")
    x2 = _post(x1, a, moba_w_o.astype(BF16), w_in, w_out, *norms, mixer_layer=0, layer=1)
    return x2[None]
```

```python
import math

import jax
import jax.numpy as jnp
from jax import lax
from jax.experimental import pallas as pl
from jax.experimental.pallas import tpu as pltpu

D_MODEL = 1024
DEPTH = 2
ROPE_THETA = 500000.0
NEG_INF = -1e30
LN_EPS = 1e-5
RMS_EPS = 1e-6
MLA_HEADS = 8
MLA_Q_RANK = 384
MLA_KV_RANK = 256
MLA_NOPE_DIM = 128
MLA_ROPE_DIM = 64
MLA_QK_DIM = MLA_NOPE_DIM + MLA_ROPE_DIM
MOBA_HEADS = 8
MOBA_HEAD_DIM = D_MODEL // MOBA_HEADS
MOBA_ROT_DIM = MOBA_HEAD_DIM // 4
MOBA_BLOCK = 256
MOBA_TOPK = 3
D_FF = 2816
DEEPNORM_ALPHA = (2 * DEPTH) ** 0.25

LANES = 128
BF16_SUBLANES = 16
LOG2E = math.log2(math.e)

HEADS = 8
QK_WIDTH = 2 * LANES
V_ROWS = LANES + BF16_SUBLANES
PROJ_ROW_TILE = 512
FFN_ROW_TILE = 512
ATTN_Q_TILE = 1024
ATTN_KEY_CHUNK = 1024
FF_CHUNKS = (768, 768, 768, 512)
VMEM_LIMIT = 56 * 1024 * 1024

BF16 = jnp.bfloat16
F32 = jnp.float32


def _dot(a, b):
    return jnp.dot(a, b, preferred_element_type=F32)


def _dot_nt(a, b):
    return lax.dot_general(a, b, (((1,), (1,)), ((), ())), preferred_element_type=F32)


def _resident(shape, index_map):
    return pl.BlockSpec(shape, index_map, pipeline_mode=pl.Buffered(1))


def _params(n_grid_dims):
    return pltpu.CompilerParams(dimension_semantics=("arbitrary",) * n_grid_dims, vmem_limit_bytes=VMEM_LIMIT)


def _qkv_out_specs(tm):
    return [pl.BlockSpec((HEADS * QK_WIDTH, tm), lambda i: (0, i)),
            pl.BlockSpec((tm, HEADS * QK_WIDTH), lambda i: (i, 0)),
            pl.BlockSpec((HEADS * V_ROWS, tm), lambda i: (0, i))]


def _qkv_out_shapes(S):
    return [jax.ShapeDtypeStruct((HEADS * QK_WIDTH, S), BF16),
            jax.ShapeDtypeStruct((S, HEADS * QK_WIDTH), BF16),
            jax.ShapeDtypeStruct((HEADS * V_ROWS, S), BF16)]


def _store_vt(vt_ref, h, v):
    tm = v.shape[0]
    r = V_ROWS * h
    vt_ref[r:r + LANES, :] = v.T.astype(BF16)
    pad_row = lax.broadcasted_iota(jnp.int32, (BF16_SUBLANES, tm), 0)
    vt_ref[r + LANES:r + V_ROWS, :] = (pad_row == 0).astype(BF16)


def _rms(x, g):
    return x * lax.rsqrt(jnp.mean(x * x, axis=-1, keepdims=True) + RMS_EPS) * g


def _angle_table(ch_ref, sh_ref, p_ref, q_ref):
    p = p_ref[...]
    q = q_ref[...]
    return jnp.concatenate([ch_ref[0, g:g + 1, :] * p + sh_ref[0, g:g + 1, :] * q
                            for g in range(ch_ref.shape[1])], axis=0)


def _mla_proj_kernel(x_ref, wd_ref, qn_ref, wq_ref, kvn_ref, wkv_ref, ch_ref, sh_ref, pc_ref, qc_ref, ps_ref, qs_ref,
                     qt_ref, k_ref, vt_ref):
    H = MLA_HEADS
    qscale = (MLA_QK_DIM ** -0.5) * LOG2E
    xb = x_ref[...].astype(BF16)
    lat = _dot(xb, wd_ref[...])
    o1 = MLA_Q_RANK
    o2 = o1 + MLA_KV_RANK
    cq = _rms(lat[:, :o1], qn_ref[...]).astype(BF16)
    ckv = _rms(lat[:, o1:o2], kvn_ref[...]).astype(BF16)
    tc = _angle_table(ch_ref, sh_ref, pc_ref, qc_ref)
    ts = _angle_table(ch_ref, sh_ref, ps_ref, qs_ref)
    kr = lat[:, o2:o2 + LANES] * tc + lat[:, o2 + LANES:o2 + 2 * LANES] * ts
    lane = lax.broadcasted_iota(jnp.int32, kr.shape, 1)
    kr_lo = jnp.where(lane < MLA_ROPE_DIM, kr, 0.0).astype(BF16)
    kr_hi = jnp.where(lane >= MLA_ROPE_DIM, kr, 0.0).astype(BF16)
    q = _dot(cq, wq_ref[...])
    kv = _dot(ckv, wkv_ref[...])
    r0 = H * MLA_NOPE_DIM
    r1 = r0 + H * MLA_ROPE_DIM
    for p in range(H // 2):
        qr = (q[:, r0 + LANES * p:r0 + LANES * (p + 1)] * tc
              + q[:, r1 + LANES * p:r1 + LANES * (p + 1)] * ts)
        qr_t = (qr * qscale).T.astype(BF16)
        for h in (2 * p, 2 * p + 1):
            c = QK_WIDTH * h
            qt_ref[c:c + LANES, :] = (q[:, LANES * h:LANES * (h + 1)] * qscale).T.astype(BF16)
            qt_ref[c + LANES:c + QK_WIDTH, :] = qr_t
            k_ref[:, c:c + LANES] = kv[:, c:c + LANES].astype(BF16)
            k_ref[:, c + LANES:c + QK_WIDTH] = kr_lo if h % 2 == 0 else kr_hi
            _store_vt(vt_ref, h, kv[:, c + LANES:c + QK_WIDTH])


def _hi_spec(tm):
    return pl.BlockSpec((1, tm // LANES, LANES), lambda i: (i, 0, 0))


def _mla_proj(x, wd, qn, wq, kvn, wkv, ch, sh, lo_tables):
    S = x.shape[0]
    tm = PROJ_ROW_TILE
    row = lambda w: pl.BlockSpec((tm, w), lambda i: (i, 0))
    full = lambda a: _resident(a.shape, lambda i: (0, 0))
    return pl.pallas_call(
        _mla_proj_kernel,
        grid=(S // tm,),
        in_specs=[row(D_MODEL), full(wd), full(qn), full(wq), full(kvn), full(wkv), _hi_spec(tm), _hi_spec(tm)]
        + [full(t) for t in lo_tables],
        out_specs=_qkv_out_specs(tm),
        out_shape=_qkv_out_shapes(S),
        compiler_params=_params(1),
        name="mla_proj",
    )(x, wd, qn, wq, kvn, wkv, ch, sh, *lo_tables)


DIAG_COLS = 256
EXP_ROWS = 16
PIPE_SLOTS = 4


def _attn_kernel(qt_ref, k_ref, vt_ref, o_ref, *scratch):
    ns = PIPE_SLOTS
    s_refs, cmax_refs, p_refs, alpha_refs = (scratch[j * ns:(j + 1) * ns] for j in range(4))
    m_ref, acc_ref = scratch[4 * ns:]
    tq = qt_ref.shape[1]
    tk = ATTN_KEY_CHUNK
    i = pl.program_id(1)
    n = (i * tq) // tk

    def chunk_start(c):
        return pl.multiple_of(jnp.clip(c, 0, n) * tk, tk)

    def stage_a(c, slot):
        s = _dot(k_ref[pl.ds(chunk_start(c), tk), :], qt_ref[...])
        s_refs[slot][...] = s
        cmax_refs[slot][...] = jnp.max(s, axis=0, keepdims=True)

    def stage_b(slot):
        s_ref = s_refs[slot]
        m_prev = m_ref[...]
        m_new = jnp.maximum(m_prev, cmax_refs[slot][...])
        alpha_refs[slot][...] = jnp.exp2(m_prev - m_new)
        m_ref[...] = m_new
        for r in range(0, tk, EXP_ROWS):
            p_refs[slot][r:r + EXP_ROWS, :] = jnp.exp2(s_ref[r:r + EXP_ROWS, :] - m_new).astype(BF16)

    def stage_c(c, slot):
        pv = _dot(vt_ref[:, pl.ds(chunk_start(c), tk)], p_refs[slot][...])
        acc_ref[...] = alpha_refs[slot][...] * acc_ref[...] + pv

    def stage_b_diag(slot):
        s_ref = s_refs[slot]
        p_ref = p_refs[slot]
        square = (lax.broadcasted_iota(jnp.int32, (DIAG_COLS, DIAG_COLS), 0)
                  <= lax.broadcasted_iota(jnp.int32, (DIAG_COLS, DIAG_COLS), 1))
        cmax = []
        for c0 in range(0, tq, DIAG_COLS):
            cols = slice(c0, c0 + DIAG_COLS)
            cm = jnp.max(jnp.where(square, s_ref[c0:c0 + DIAG_COLS, cols], NEG_INF), axis=0, keepdims=True)
            if c0:
                cm = jnp.maximum(cm, jnp.max(s_ref[0:c0, cols], axis=0, keepdims=True))
            cmax.append(cm)
        m_prev = m_ref[...]
        m_new = jnp.maximum(m_prev, jnp.concatenate(cmax, axis=1))
        alpha_refs[slot][...] = jnp.exp2(m_prev - m_new)
        m_ref[...] = m_new
        row = lax.broadcasted_iota(jnp.int32, (EXP_ROWS, DIAG_COLS), 0)
        col = lax.broadcasted_iota(jnp.int32, (EXP_ROWS, DIAG_COLS), 1)
        for c0 in range(0, tq, DIAG_COLS):
            cols = slice(c0, c0 + DIAG_COLS)
            m_cols = m_new[:, cols]
            for r in range(0, c0, EXP_ROWS):
                p_ref[r:r + EXP_ROWS, cols] = jnp.exp2(s_ref[r:r + EXP_ROWS, cols] - m_cols).astype(BF16)
            for r in range(c0, c0 + DIAG_COLS, EXP_ROWS):
                s = jnp.where(row + (r - c0) <= col, s_ref[r:r + EXP_ROWS, cols], NEG_INF)
                p_ref[r:r + EXP_ROWS, cols] = jnp.exp2(s - m_cols).astype(BF16)

    def stage_c_diag(c, slot):
        start = chunk_start(c)
        for c0 in range(0, tq, DIAG_COLS):
            cols = slice(c0, c0 + DIAG_COLS)
            keys = c0 + DIAG_COLS
            pv = _dot(vt_ref[:, pl.ds(start, keys)], p_refs[slot][0:keys, cols])
            acc_ref[:, cols] = alpha_refs[slot][:, cols] * acc_ref[:, cols] + pv

    def group(u, x, y, with_c=True):
        if with_c:
            stage_c(2 * u - 2, y[0])
            stage_c(2 * u - 1, y[1])
        stage_b(x[0])
        stage_b(x[1])
        stage_a(2 * u + 2, y[0])
        stage_a(2 * u + 3, y[1])

    def finish(x, y, n_is_odd, with_c=True):
        if with_c:
            stage_c(n - n % 2 - 2, y[0])
            stage_c(n - n % 2 - 1, y[1])
        if n_is_odd:
            stage_b(x[0])
            stage_b_diag(x[1])
            stage_c(n - 1, x[0])
            stage_c_diag(n, x[1])
        else:
            stage_b_diag(x[0])
            stage_c_diag(n, x[0])
        acc = acc_ref[...]
        o_ref[...] = (acc[:LANES] / acc[LANES:LANES + 1]).T.astype(o_ref.dtype)

    lo, hi = (0, 1), (2, 3)
    m_ref[...] = jnp.full(m_ref.shape, -jnp.inf, F32)
    acc_ref[...] = jnp.zeros_like(acc_ref)
    stage_a(0, lo[0])
    stage_a(1, lo[1])

    @pl.when(n >= 2)
    def _():
        group(0, lo, hi, with_c=False)

    def body(u, carry):
        @pl.when(u % 2 == 0)
        def _():
            group(u, lo, hi)

        @pl.when(u % 2 == 1)
        def _():
            group(u, hi, lo)

        return carry

    lax.fori_loop(1, n // 2, body, 0)
    for r in range(2):
        @pl.when(n == r)
        def _(r=r):
            finish(lo, hi, n_is_odd=bool(r), with_c=False)

    for r in range(4):
        @pl.when((n >= 2) & (n % 4 == r))
        def _(r=r):
            x, y = (lo, hi) if r < 2 else (hi, lo)
            finish(x, y, n_is_odd=bool(r % 2))


def _attention(qt, k, vt, *, name):
    S = k.shape[0]
    tq = ATTN_Q_TILE
    tk = ATTN_KEY_CHUNK
    return pl.pallas_call(
        _attn_kernel,
        grid=(HEADS, S // tq),
        in_specs=[pl.BlockSpec((QK_WIDTH, tq), lambda h, i: (h, i)),
                  pl.BlockSpec((S, QK_WIDTH), lambda h, i: (0, h)),
                  pl.BlockSpec((V_ROWS, S), lambda h, i: (h, 0))],
        out_specs=pl.BlockSpec((tq, LANES), lambda h, i: (i, h)),
        out_shape=jax.ShapeDtypeStruct((S, LANES * HEADS), BF16),
        scratch_shapes=[pltpu.VMEM((tk, tq), F32)] * PIPE_SLOTS
        + [pltpu.VMEM((1, tq), F32)] * PIPE_SLOTS
        + [pltpu.VMEM((tk, tq), BF16)] * PIPE_SLOTS
        + [pltpu.VMEM((1, tq), F32)] * PIPE_SLOTS
        + [pltpu.VMEM((1, tq), F32),
           pltpu.VMEM((V_ROWS, tq), F32)],
        compiler_params=_params(2),
        name=name,
    )(qt, k, vt)


def _layer_norm(z, g, b):
    mu = jnp.mean(z, axis=-1, keepdims=True)
    d = z - mu
    var = jnp.mean(d * d, axis=-1, keepdims=True)
    return d * lax.rsqrt(var + LN_EPS) * g + b


def _post_kernel(x_ref, a_ref, wo_ref, win_ref, wout_ref, g1_ref, b1_ref, g2_ref, b2_ref, o_ref):
    h = _dot(a_ref[...], wo_ref[...])
    x1 = _layer_norm(DEEPNORM_ALPHA * x_ref[...] + h, g1_ref[...], b1_ref[...])
    x1b = x1.astype(BF16)
    y = None
    off = 0
    for c in FF_CHUNKS:
        gate = _dot(x1b, win_ref[:, off:off + c])
        up = _dot(x1b, win_ref[:, D_FF + off:D_FF + off + c])
        act = (gate * jax.nn.sigmoid(gate) * up).astype(BF16)
        part = _dot(act, wout_ref[off:off + c, :])
        y = part if y is None else y + part
        off += c
    o_ref[...] = _layer_norm(DEEPNORM_ALPHA * x1 + y, g2_ref[...], b2_ref[...])


def _post(x, a, wo, win, wout, g1, b1, g2, b2, *, mixer_layer, layer):
    S = x.shape[0]
    tm = FFN_ROW_TILE
    row = pl.BlockSpec((tm, D_MODEL), lambda i: (i, 0))

    def layer_of(arr, l):
        return _resident((None,) + arr.shape[1:], lambda i: (l,) + (0,) * (arr.ndim - 1))

    return pl.pallas_call(
        _post_kernel,
        grid=(S // tm,),
        in_specs=[row, row, layer_of(wo, mixer_layer)] + [layer_of(t, layer) for t in (win, wout, g1, b1, g2, b2)],
        out_specs=row,
        out_shape=jax.ShapeDtypeStruct((S, D_MODEL), F32),
        compiler_params=_params(1),
        name="post",
    )(x, a, wo, win, wout, g1, b1, g2, b2)


def _moba_proj_kernel(x_ref, w_ref, ch_ref, sh_ref, pc_ref, qc_ref, pa_ref, qa_ref, pb_ref, qb_ref,
                      qt_ref, k_ref, vt_ref, kmean_ref):
    H = MOBA_HEADS
    tm = x_ref.shape[0]
    t = pl.program_id(0)
    qscale = (MOBA_HEAD_DIM ** -0.5) * LOG2E
    half = MOBA_ROT_DIM // 2

    @pl.when(t == 0)
    def _():
        kmean_ref[...] = jnp.zeros_like(kmean_ref)

    qkv = _dot(x_ref[...].astype(BF16), w_ref[...])
    cc = _angle_table(ch_ref, sh_ref, pc_ref, qc_ref)
    sa = _angle_table(ch_ref, sh_ref, pa_ref, qa_ref)
    sb = _angle_table(ch_ref, sh_ref, pb_ref, qb_ref)

    def rope(z):
        return z * cc + pltpu.roll(z, LANES - half, 1) * sa + pltpu.roll(z, half, 1) * sb

    blocks = tm // MOBA_BLOCK
    nb = kmean_ref.shape[0]
    nblk = lax.broadcasted_iota(jnp.int32, (nb, tm), 0).astype(F32)
    qpos = t * tm + lax.broadcasted_iota(jnp.int32, (nb, tm), 1)
    own = (qpos // MOBA_BLOCK).astype(F32)
    past = nblk < own
    kmean_row = lax.broadcasted_iota(jnp.int32, (nb, LANES), 0)
    key_blk = (t * tm + lax.broadcasted_iota(jnp.int32, (tm, LANES), 0)) // MOBA_BLOCK
    blk_onehot = (key_blk == lax.broadcasted_iota(jnp.int32, (tm, LANES), 1)).astype(BF16)
    never = jnp.full((LANES - nb, tm), NEG_INF, BF16)
    for h in range(H):
        c = LANES * h
        qh = rope(qkv[:, c:c + LANES])
        kh = rope(qkv[:, H * LANES + c:H * LANES + c + LANES])
        k_ref[:, 2 * c:2 * c + LANES] = kh.astype(BF16)
        k_ref[:, 2 * c + LANES:2 * c + QK_WIDTH] = blk_onehot
        _store_vt(vt_ref, h, qkv[:, 2 * H * LANES + c:2 * H * LANES + c + LANES])
        km = kmean_ref[:, c:c + LANES]
        for b in range(blocks):
            mean_b = jnp.mean(kh[b * MOBA_BLOCK:(b + 1) * MOBA_BLOCK], axis=0, keepdims=True)
            km = jnp.where(kmean_row == t * blocks + b, mean_b, km)
        kmean_ref[:, c:c + LANES] = km
        km_hi = km.astype(BF16)
        km_lo = (km - km_hi.astype(F32)).astype(BF16)
        q_hi = qh.astype(BF16)
        q_lo = (qh - q_hi.astype(F32)).astype(BF16)
        g = _dot_nt(km_hi, q_hi) + (_dot_nt(km_hi, q_lo) + _dot_nt(km_lo, q_hi))
        g = jnp.where(past, g, NEG_INF)
        for _ in range(MOBA_TOPK):
            best = jnp.max(g, axis=0, keepdims=True)
            idx = jnp.min(jnp.where(g == best, nblk, float(nb)), axis=0, keepdims=True)
            g = jnp.where(nblk == idx, -jnp.inf, g)
        sel = ((g == -jnp.inf) & past) | (nblk == own)
        qt_ref[2 * c:2 * c + LANES, :] = (qh * qscale).T.astype(BF16)
        qt_ref[2 * c + LANES:2 * c + LANES + nb, :] = jnp.where(sel, 0.0, NEG_INF).astype(BF16)
        qt_ref[2 * c + LANES + nb:2 * c + QK_WIDTH, :] = never


def _moba_proj(x, w, ch, sh, lo_tables):
    S = x.shape[0]
    tm = PROJ_ROW_TILE
    row = lambda wd: pl.BlockSpec((tm, wd), lambda i: (i, 0))
    return pl.pallas_call(
        _moba_proj_kernel,
        grid=(S // tm,),
        in_specs=[row(D_MODEL), _resident(w.shape, lambda i: (0, 0)), _hi_spec(tm), _hi_spec(tm)]
        + [_resident(t.shape, lambda i: (0, 0)) for t in lo_tables],
        out_specs=_qkv_out_specs(tm),
        out_shape=_qkv_out_shapes(S),
        scratch_shapes=[pltpu.VMEM((S // MOBA_BLOCK, LANES * MOBA_HEADS), F32)],
        compiler_params=_params(1),
        name="moba_proj",
    )(x, w, ch, sh, *lo_tables)


def _rotary_factors(seq_len, tm, lane_freq, combos):
    lo = jnp.arange(LANES, dtype=F32)[:, None] * lane_freq[None, :]
    hi = (jnp.arange(seq_len // LANES, dtype=F32) * LANES)[:, None] * lane_freq[None, :]
    cl, sl = jnp.cos(lo), jnp.sin(lo)
    shape = (seq_len // tm, tm // LANES, LANES)
    lo_tables = []
    for a, b in combos:
        lo_tables += [a[None, :] * cl + b[None, :] * sl, b[None, :] * cl - a[None, :] * sl]
    return jnp.cos(hi).reshape(shape), jnp.sin(hi).reshape(shape), lo_tables


def _swap_halves(w, width):
    shp = w.shape
    w = w.reshape(shp[0], -1, 2, width // 2)
    return w[:, :, ::-1, :].reshape(shp)


def kernel(x, mla_w_dqkv, mla_q_norm, mla_w_uq, mla_kv_norm, mla_w_ukv, mla_w_o, moba_w_qkv, moba_w_o,
           ffn_w_in, ffn_w_out, ln_mix_g, ln_mix_b, ln_ffn_g, ln_ffn_b):
    B, S, D = x.shape
    assert B == 1 and D == D_MODEL and S % ATTN_KEY_CHUNK == 0 and S // MOBA_BLOCK <= LANES and (S // MOBA_BLOCK) % BF16_SUBLANES == 0
    assert MLA_HEADS == HEADS and MOBA_HEADS == HEADS and ATTN_KEY_CHUNK == ATTN_Q_TILE
    xs = x[0]
    H = MLA_HEADS

    lane = jnp.arange(LANES)
    zero, one = jnp.zeros((LANES,), F32), jnp.ones((LANES,), F32)
    f_mla = ROPE_THETA ** (-jnp.arange(0, MLA_ROPE_DIM, 2, dtype=F32) / MLA_ROPE_DIM)
    hr = MLA_ROPE_DIM // 2
    sign = jnp.where(lane % MLA_ROPE_DIM < hr, -1.0, 1.0).astype(F32)
    mla_tables = _rotary_factors(S, PROJ_ROW_TILE, jnp.tile(f_mla, LANES // hr), [(one, zero), (zero, sign)])
    f_moba = ROPE_THETA ** (-jnp.arange(0, MOBA_ROT_DIM, 2, dtype=F32) / MOBA_ROT_DIM)
    hr = MOBA_ROT_DIM // 2
    rot = lane < MOBA_ROT_DIM
    first = (lane < hr).astype(F32)
    second = (rot & (lane >= hr)).astype(F32)
    moba_tables = _rotary_factors(S, PROJ_ROW_TILE, jnp.where(rot, jnp.tile(f_moba, LANES // hr), 0.0),
                                  [(one, zero), (zero, -first), (zero, second)])

    wd = mla_w_dqkv[0]
    w_kr = wd[:, MLA_Q_RANK + MLA_KV_RANK:]
    w_kr_sw = _swap_halves(w_kr, MLA_ROPE_DIM)
    wd_ext = jnp.concatenate([wd[:, :MLA_Q_RANK + MLA_KV_RANK], w_kr, w_kr, w_kr_sw, w_kr_sw], axis=1).astype(BF16)
    wq = mla_w_uq[0].reshape(MLA_Q_RANK, H, MLA_QK_DIM)
    wq_nope = wq[:, :, :MLA_NOPE_DIM].reshape(MLA_Q_RANK, H * MLA_NOPE_DIM)
    wq_rope = wq[:, :, MLA_NOPE_DIM:].reshape(MLA_Q_RANK, H * MLA_ROPE_DIM)
    wq_ext = jnp.concatenate([wq_nope, wq_rope, _swap_halves(wq_rope, MLA_ROPE_DIM)], axis=1).astype(BF16)

    qt, k, vt = _mla_proj(xs, wd_ext, mla_q_norm[0][None, :], wq_ext, mla_kv_norm[0][None, :],
                          mla_w_ukv[0].astype(BF16), *mla_tables)
    w_in, w_out = ffn_w_in.astype(BF16), ffn_w_out.astype(BF16)
    norms = [p[:, None, :] for p in (ln_mix_g, ln_mix_b, ln_ffn_g, ln_ffn_b)]

    a = _attention(qt, k, vt, name="mla_attn")
    x1 = _post(xs, a, mla_w_o.astype(BF16), w_in, w_out, *norms, mixer_layer=0, layer=0)

    qt, k, vt = _moba_proj(x1, moba_w_qkv[0].astype(BF16), *moba_tables)
    a = _attention(qt, k, vt, name="moba_attn")
    x2 = _post(x1, a, moba_w_o.astype(BF16), w_in, w_out, *norms, mixer_layer=0, layer=1)
    return x2[None]
```

```python
import math

import jax
import jax.numpy as jnp
from jax import lax
from jax.experimental import pallas as pl
from jax.experimental.pallas import tpu as pltpu

D_MODEL = 1024
DEPTH = 2
ROPE_THETA = 500000.0
NEG_INF = -1e30
LN_EPS = 1e-5
RMS_EPS = 1e-6
MLA_HEADS = 8
MLA_Q_RANK = 384
MLA_KV_RANK = 256
MLA_NOPE_DIM = 128
MLA_ROPE_DIM = 64
MLA_QK_DIM = MLA_NOPE_DIM + MLA_ROPE_DIM
MOBA_HEADS = 8
MOBA_HEAD_DIM = D_MODEL // MOBA_HEADS
MOBA_ROT_DIM = MOBA_HEAD_DIM // 4
MOBA_BLOCK = 256
MOBA_TOPK = 3
D_FF = 2816
DEEPNORM_ALPHA = (2 * DEPTH) ** 0.25

LANES = 128
BF16_SUBLANES = 16
LOG2E = math.log2(math.e)

HEADS = 8
QK_WIDTH = 2 * LANES
V_ROWS = LANES + BF16_SUBLANES
PROJ_ROW_TILE = 512
MIX_ROW_TILE = 1024
FFN_ROW_TILE = 1024
ATTN_Q_TILE = 1024
ATTN_KEY_CHUNK = 1024
FF_CHUNKS = (768, 768, 768, 512)
VMEM_LIMIT = 56 * 1024 * 1024

BF16 = jnp.bfloat16
F32 = jnp.float32


def _dot(a, b):
    return jnp.dot(a, b, preferred_element_type=F32)


def _dot_nt(a, b):
    return lax.dot_general(a, b, (((1,), (1,)), ((), ())), preferred_element_type=F32)


def _resident(shape, index_map):
    return pl.BlockSpec(shape, index_map, pipeline_mode=pl.Buffered(1))


def _params(n_grid_dims):
    return pltpu.CompilerParams(dimension_semantics=("arbitrary",) * n_grid_dims, vmem_limit_bytes=VMEM_LIMIT)


def _qkv_out_specs(tm):
    return [pl.BlockSpec((HEADS * QK_WIDTH, tm), lambda i: (0, i)),
            pl.BlockSpec((tm, HEADS * QK_WIDTH), lambda i: (i, 0)),
            pl.BlockSpec((HEADS * V_ROWS, tm), lambda i: (0, i))]


def _qkv_out_shapes(S):
    return [jax.ShapeDtypeStruct((HEADS * QK_WIDTH, S), BF16),
            jax.ShapeDtypeStruct((S, HEADS * QK_WIDTH), BF16),
            jax.ShapeDtypeStruct((HEADS * V_ROWS, S), BF16)]


def _store_vt(vt_ref, h, v):
    tm = v.shape[0]
    r = V_ROWS * h
    vt_ref[r:r + LANES, :] = v.T.astype(BF16)
    pad_row = lax.broadcasted_iota(jnp.int32, (BF16_SUBLANES, tm), 0)
    vt_ref[r + LANES:r + V_ROWS, :] = (pad_row == 0).astype(BF16)


def _rms(x, g):
    return x * lax.rsqrt(jnp.mean(x * x, axis=-1, keepdims=True) + RMS_EPS) * g


def _angle_table(ch_ref, sh_ref, p_ref, q_ref):
    p = p_ref[...]
    q = q_ref[...]
    return jnp.concatenate([ch_ref[0, g:g + 1, :] * p + sh_ref[0, g:g + 1, :] * q
                            for g in range(ch_ref.shape[1])], axis=0)


def _mla_proj_kernel(x_ref, wd_ref, qn_ref, wq_ref, kvn_ref, wkv_ref, ch_ref, sh_ref, pc_ref, qc_ref, ps_ref, qs_ref,
                     qt_ref, k_ref, vt_ref):
    H = MLA_HEADS
    qscale = (MLA_QK_DIM ** -0.5) * LOG2E
    xb = x_ref[...].astype(BF16)
    lat = _dot(xb, wd_ref[...])
    o1 = MLA_Q_RANK
    o2 = o1 + MLA_KV_RANK
    cq = _rms(lat[:, :o1], qn_ref[...]).astype(BF16)
    ckv = _rms(lat[:, o1:o2], kvn_ref[...]).astype(BF16)
    tc = _angle_table(ch_ref, sh_ref, pc_ref, qc_ref)
    ts = _angle_table(ch_ref, sh_ref, ps_ref, qs_ref)
    kr = lat[:, o2:o2 + LANES] * tc + lat[:, o2 + LANES:o2 + 2 * LANES] * ts
    lane = lax.broadcasted_iota(jnp.int32, kr.shape, 1)
    kr_lo = jnp.where(lane < MLA_ROPE_DIM, kr, 0.0).astype(BF16)
    kr_hi = jnp.where(lane >= MLA_ROPE_DIM, kr, 0.0).astype(BF16)
    q = _dot(cq, wq_ref[...])
    kv = _dot(ckv, wkv_ref[...])
    r0 = H * MLA_NOPE_DIM
    r1 = r0 + H * MLA_ROPE_DIM
    for p in range(H // 2):
        qr = (q[:, r0 + LANES * p:r0 + LANES * (p + 1)] * tc
              + q[:, r1 + LANES * p:r1 + LANES * (p + 1)] * ts)
        qr_t = (qr * qscale).T.astype(BF16)
        for h in (2 * p, 2 * p + 1):
            c = QK_WIDTH * h
            qt_ref[c:c + LANES, :] = (q[:, LANES * h:LANES * (h + 1)] * qscale).T.astype(BF16)
            qt_ref[c + LANES:c + QK_WIDTH, :] = qr_t
            k_ref[:, c:c + LANES] = kv[:, c:c + LANES].astype(BF16)
            k_ref[:, c + LANES:c + QK_WIDTH] = kr_lo if h % 2 == 0 else kr_hi
            _store_vt(vt_ref, h, kv[:, c + LANES:c + QK_WIDTH])


def _hi_spec(tm):
    return pl.BlockSpec((1, tm // LANES, LANES), lambda i: (i, 0, 0))


def _mla_proj(x, wd, qn, wq, kvn, wkv, ch, sh, lo_tables):
    S = x.shape[0]
    tm = PROJ_ROW_TILE
    row = lambda w: pl.BlockSpec((tm, w), lambda i: (i, 0))
    full = lambda a: _resident(a.shape, lambda i: (0, 0))
    return pl.pallas_call(
        _mla_proj_kernel,
        grid=(S // tm,),
        in_specs=[row(D_MODEL), full(wd), full(qn), full(wq), full(kvn), full(wkv), _hi_spec(tm), _hi_spec(tm)]
        + [full(t) for t in lo_tables],
        out_specs=_qkv_out_specs(tm),
        out_shape=_qkv_out_shapes(S),
        compiler_params=_params(1),
        name="mla_proj",
    )(x, wd, qn, wq, kvn, wkv, ch, sh, *lo_tables)


DIAG_COLS = 256
EXP_ROWS = 16
PIPE_SLOTS = 4


def _attn_kernel(qt_ref, k_ref, vt_ref, o_ref, *scratch):
    ns = PIPE_SLOTS
    s_refs, cmax_refs, p_refs, alpha_refs = (scratch[j * ns:(j + 1) * ns] for j in range(4))
    m_ref, acc_ref = scratch[4 * ns:]
    tq = qt_ref.shape[1]
    tk = ATTN_KEY_CHUNK
    i = pl.program_id(1)
    n = (i * tq) // tk

    def chunk_start(c):
        return pl.multiple_of(jnp.clip(c, 0, n) * tk, tk)

    def stage_a(c, slot):
        s = _dot(k_ref[pl.ds(chunk_start(c), tk), :], qt_ref[...])
        s_refs[slot][...] = s
        cmax_refs[slot][...] = jnp.max(s, axis=0, keepdims=True)

    def stage_b(slot):
        s_ref = s_refs[slot]
        m_prev = m_ref[...]
        m_new = jnp.maximum(m_prev, cmax_refs[slot][...])
        alpha_refs[slot][...] = jnp.exp2(m_prev - m_new)
        m_ref[...] = m_new
        for r in range(0, tk, EXP_ROWS):
            p_refs[slot][r:r + EXP_ROWS, :] = jnp.exp2(s_ref[r:r + EXP_ROWS, :] - m_new).astype(BF16)

    def stage_c(c, slot):
        pv = _dot(vt_ref[:, pl.ds(chunk_start(c), tk)], p_refs[slot][...])
        acc_ref[...] = alpha_refs[slot][...] * acc_ref[...] + pv

    def stage_b_diag(slot):
        s_ref = s_refs[slot]
        p_ref = p_refs[slot]
        square = (lax.broadcasted_iota(jnp.int32, (DIAG_COLS, DIAG_COLS), 0)
                  <= lax.broadcasted_iota(jnp.int32, (DIAG_COLS, DIAG_COLS), 1))
        cmax = []
        for c0 in range(0, tq, DIAG_COLS):
            cols = slice(c0, c0 + DIAG_COLS)
            cm = jnp.max(jnp.where(square, s_ref[c0:c0 + DIAG_COLS, cols], NEG_INF), axis=0, keepdims=True)
            if c0:
                cm = jnp.maximum(cm, jnp.max(s_ref[0:c0, cols], axis=0, keepdims=True))
            cmax.append(cm)
        m_prev = m_ref[...]
        m_new = jnp.maximum(m_prev, jnp.concatenate(cmax, axis=1))
        alpha_refs[slot][...] = jnp.exp2(m_prev - m_new)
        m_ref[...] = m_new
        row = lax.broadcasted_iota(jnp.int32, (EXP_ROWS, DIAG_COLS), 0)
        col = lax.broadcasted_iota(jnp.int32, (EXP_ROWS, DIAG_COLS), 1)
        for c0 in range(0, tq, DIAG_COLS):
            cols = slice(c0, c0 + DIAG_COLS)
            m_cols = m_new[:, cols]
            for r in range(0, c0, EXP_ROWS):
                p_ref[r:r + EXP_ROWS, cols] = jnp.exp2(s_ref[r:r + EXP_ROWS, cols] - m_cols).astype(BF16)
            for r in range(c0, c0 + DIAG_COLS, EXP_ROWS):
                s = jnp.where(row + (r - c0) <= col, s_ref[r:r + EXP_ROWS, cols], NEG_INF)
                p_ref[r:r + EXP_ROWS, cols] = jnp.exp2(s - m_cols).astype(BF16)

    def stage_c_diag(c, slot):
        start = chunk_start(c)
        for c0 in range(0, tq, DIAG_COLS):
            cols = slice(c0, c0 + DIAG_COLS)
            keys = c0 + DIAG_COLS
            pv = _dot(vt_ref[:, pl.ds(start, keys)], p_refs[slot][0:keys, cols])
            acc_ref[:, cols] = alpha_refs[slot][:, cols] * acc_ref[:, cols] + pv

    def group(u, x, y, with_c=True):
        if with_c:
            stage_c(2 * u - 2, y[0])
            stage_c(2 * u - 1, y[1])
        stage_b(x[0])
        stage_b(x[1])
        stage_a(2 * u + 2, y[0])
        stage_a(2 * u + 3, y[1])

    def finish(x, y, n_is_odd, with_c=True):
        if with_c:
            stage_c(n - n % 2 - 2, y[0])
            stage_c(n - n % 2 - 1, y[1])
        if n_is_odd:
            stage_b(x[0])
            stage_b_diag(x[1])
            stage_c(n - 1, x[0])
            stage_c_diag(n, x[1])
        else:
            stage_b_diag(x[0])
            stage_c_diag(n, x[0])
        acc = acc_ref[...]
        o_ref[...] = (acc[:LANES] / acc[LANES:LANES + 1]).T.astype(o_ref.dtype)

    lo, hi = (0, 1), (2, 3)
    m_ref[...] = jnp.full(m_ref.shape, -jnp.inf, F32)
    acc_ref[...] = jnp.zeros_like(acc_ref)
    stage_a(0, lo[0])
    stage_a(1, lo[1])

    @pl.when(n >= 2)
    def _():
        group(0, lo, hi, with_c=False)

    def body(u, carry):
        @pl.when(u % 2 == 0)
        def _():
            group(u, lo, hi)

        @pl.when(u % 2 == 1)
        def _():
            group(u, hi, lo)

        return carry

    lax.fori_loop(1, n // 2, body, 0)
    for r in range(2):
        @pl.when(n == r)
        def _(r=r):
            finish(lo, hi, n_is_odd=bool(r), with_c=False)

    for r in range(4):
        @pl.when((n >= 2) & (n % 4 == r))
        def _(r=r):
            x, y = (lo, hi) if r < 2 else (hi, lo)
            finish(x, y, n_is_odd=bool(r % 2))


def _attention(qt, k, vt, *, name):
    S = k.shape[0]
    tq = ATTN_Q_TILE
    tk = ATTN_KEY_CHUNK
    return pl.pallas_call(
        _attn_kernel,
        grid=(HEADS, S // tq),
        in_specs=[pl.BlockSpec((QK_WIDTH, tq), lambda h, i: (h, i)),
                  pl.BlockSpec((S, QK_WIDTH), lambda h, i: (0, h)),
                  pl.BlockSpec((V_ROWS, S), lambda h, i: (h, 0))],
        out_specs=pl.BlockSpec((tq, LANES), lambda h, i: (i, h)),
        out_shape=jax.ShapeDtypeStruct((S, LANES * HEADS), BF16),
        scratch_shapes=[pltpu.VMEM((tk, tq), F32)] * PIPE_SLOTS
        + [pltpu.VMEM((1, tq), F32)] * PIPE_SLOTS
        + [pltpu.VMEM((tk, tq), BF16)] * PIPE_SLOTS
        + [pltpu.VMEM((1, tq), F32)] * PIPE_SLOTS
        + [pltpu.VMEM((1, tq), F32),
           pltpu.VMEM((V_ROWS, tq), F32)],
        compiler_params=_params(2),
        name=name,
    )(qt, k, vt)


def _layer_norm(z, g, b):
    mu = jnp.mean(z, axis=-1, keepdims=True)
    d = z - mu
    var = jnp.mean(d * d, axis=-1, keepdims=True)
    return d * lax.rsqrt(var + LN_EPS) * g + b


def _mix_out_kernel(x_ref, a_ref, wo_ref, g1_ref, b1_ref, o_ref):
    h = _dot(a_ref[...], wo_ref[...])
    o_ref[...] = _layer_norm(DEEPNORM_ALPHA * x_ref[...] + h, g1_ref[...], b1_ref[...])


def _ffn_kernel(x1_ref, win_ref, wout_ref, g2_ref, b2_ref, o_ref):
    x1 = x1_ref[...]
    x1b = x1.astype(BF16)
    y = None
    off = 0
    for c in FF_CHUNKS:
        gate = _dot(x1b, win_ref[:, off:off + c])
        up = _dot(x1b, win_ref[:, D_FF + off:D_FF + off + c])
        act = (gate * jax.nn.sigmoid(gate) * up).astype(BF16)
        part = _dot(act, wout_ref[off:off + c, :])
        y = part if y is None else y + part
        off += c
    o_ref[...] = _layer_norm(DEEPNORM_ALPHA * x1 + y, g2_ref[...], b2_ref[...])


def _post(x, a, wo, win, wout, g1, b1, g2, b2, *, mixer_layer, layer):
    S = x.shape[0]

    def layer_of(arr, l):
        return _resident((None,) + arr.shape[1:], lambda i: (l,) + (0,) * (arr.ndim - 1))

    def call(kernel, tm, name, row_args, params):
        row = pl.BlockSpec((tm, D_MODEL), lambda i: (i, 0))
        return pl.pallas_call(
            kernel,
            grid=(S // tm,),
            in_specs=[row] * len(row_args) + [layer_of(t, l) for t, l in params],
            out_specs=row,
            out_shape=jax.ShapeDtypeStruct((S, D_MODEL), F32),
            compiler_params=_params(1),
            name=name,
        )(*row_args, *[t for t, _ in params])

    x1 = call(_mix_out_kernel, MIX_ROW_TILE, "mix_out", (x, a), [(wo, mixer_layer), (g1, layer), (b1, layer)])
    return call(_ffn_kernel, FFN_ROW_TILE, "ffn", (x1,), [(win, layer), (wout, layer), (g2, layer), (b2, layer)])


def _moba_proj_kernel(x_ref, w_ref, ch_ref, sh_ref, pc_ref, qc_ref, pa_ref, qa_ref, pb_ref, qb_ref,
                      qt_ref, k_ref, vt_ref, kmean_ref):
    H = MOBA_HEADS
    tm = x_ref.shape[0]
    t = pl.program_id(0)
    qscale = (MOBA_HEAD_DIM ** -0.5) * LOG2E
    half = MOBA_ROT_DIM // 2

    @pl.when(t == 0)
    def _():
        kmean_ref[...] = jnp.zeros_like(kmean_ref)

    qkv = _dot(x_ref[...].astype(BF16), w_ref[...])
    cc = _angle_table(ch_ref, sh_ref, pc_ref, qc_ref)
    sa = _angle_table(ch_ref, sh_ref, pa_ref, qa_ref)
    sb = _angle_table(ch_ref, sh_ref, pb_ref, qb_ref)

    def rope(z):
        return z * cc + pltpu.roll(z, LANES - half, 1) * sa + pltpu.roll(z, half, 1) * sb

    blocks = tm // MOBA_BLOCK
    nb = kmean_ref.shape[0]
    nblk = lax.broadcasted_iota(jnp.int32, (nb, tm), 0).astype(F32)
    qpos = t * tm + lax.broadcasted_iota(jnp.int32, (nb, tm), 1)
    own = (qpos // MOBA_BLOCK).astype(F32)
    past = nblk < own
    kmean_row = lax.broadcasted_iota(jnp.int32, (nb, LANES), 0)
    key_blk = (t * tm + lax.broadcasted_iota(jnp.int32, (tm, LANES), 0)) // MOBA_BLOCK
    blk_onehot = (key_blk == lax.broadcasted_iota(jnp.int32, (tm, LANES), 1)).astype(BF16)
    never = jnp.full((LANES - nb, tm), NEG_INF, BF16)
    for h in range(H):
        c = LANES * h
        qh = rope(qkv[:, c:c + LANES])
        kh = rope(qkv[:, H * LANES + c:H * LANES + c + LANES])
        k_ref[:, 2 * c:2 * c + LANES] = kh.astype(BF16)
        k_ref[:, 2 * c + LANES:2 * c + QK_WIDTH] = blk_onehot
        _store_vt(vt_ref, h, qkv[:, 2 * H * LANES + c:2 * H * LANES + c + LANES])
        km = kmean_ref[:, c:c + LANES]
        for b in range(blocks):
            mean_b = jnp.mean(kh[b * MOBA_BLOCK:(b + 1) * MOBA_BLOCK], axis=0, keepdims=True)
            km = jnp.where(kmean_row == t * blocks + b, mean_b, km)
        kmean_ref[:, c:c + LANES] = km
        km_hi = km.astype(BF16)
        km_lo = (km - km_hi.astype(F32)).astype(BF16)
        q_hi = qh.astype(BF16)
        q_lo = (qh - q_hi.astype(F32)).astype(BF16)
        g = _dot_nt(km_hi, q_hi) + (_dot_nt(km_hi, q_lo) + _dot_nt(km_lo, q_hi))
        g = jnp.where(past, g, NEG_INF)
        for _ in range(MOBA_TOPK):
            best = jnp.max(g, axis=0, keepdims=True)
            idx = jnp.min(jnp.where(g == best, nblk, float(nb)), axis=0, keepdims=True)
            g = jnp.where(nblk == idx, -jnp.inf, g)
        sel = ((g == -jnp.inf) & past) | (nblk == own)
        qt_ref[2 * c:2 * c + LANES, :] = (qh * qscale).T.astype(BF16)
        qt_ref[2 * c + LANES:2 * c + LANES + nb, :] = jnp.where(sel, 0.0, NEG_INF).astype(BF16)
        qt_ref[2 * c + LANES + nb:2 * c + QK_WIDTH, :] = never


def _moba_proj(x, w, ch, sh, lo_tables):
    S = x.shape[0]
    tm = PROJ_ROW_TILE
    row = lambda wd: pl.BlockSpec((tm, wd), lambda i: (i, 0))
    return pl.pallas_call(
        _moba_proj_kernel,
        grid=(S // tm,),
        in_specs=[row(D_MODEL), _resident(w.shape, lambda i: (0, 0)), _hi_spec(tm), _hi_spec(tm)]
        + [_resident(t.shape, lambda i: (0, 0)) for t in lo_tables],
        out_specs=_qkv_out_specs(tm),
        out_shape=_qkv_out_shapes(S),
        scratch_shapes=[pltpu.VMEM((S // MOBA_BLOCK, LANES * MOBA_HEADS), F32)],
        compiler_params=_params(1),
        name="moba_proj",
    )(x, w, ch, sh, *lo_tables)


def _rotary_factors(seq_len, tm, lane_freq, combos):
    lo = jnp.arange(LANES, dtype=F32)[:, None] * lane_freq[None, :]
    hi = (jnp.arange(seq_len // LANES, dtype=F32) * LANES)[:, None] * lane_freq[None, :]
    cl, sl = jnp.cos(lo), jnp.sin(lo)
    shape = (seq_len // tm, tm // LANES, LANES)
    lo_tables = []
    for a, b in combos:
        lo_tables += [a[None, :] * cl + b[None, :] * sl, b[None, :] * cl - a[None, :] * sl]
    return jnp.cos(hi).reshape(shape), jnp.sin(hi).reshape(shape), lo_tables


def _swap_halves(w, width):
    shp = w.shape
    w = w.reshape(shp[0], -1, 2, width // 2)
    return w[:, :, ::-1, :].reshape(shp)


def kernel(x, mla_w_dqkv, mla_q_norm, mla_w_uq, mla_kv_norm, mla_w_ukv, mla_w_o, moba_w_qkv, moba_w_o,
           ffn_w_in, ffn_w_out, ln_mix_g, ln_mix_b, ln_ffn_g, ln_ffn_b):
    B, S, D = x.shape
    assert B == 1 and D == D_MODEL and S % ATTN_KEY_CHUNK == 0 and S // MOBA_BLOCK <= LANES and (S // MOBA_BLOCK) % BF16_SUBLANES == 0
    assert MLA_HEADS == HEADS and MOBA_HEADS == HEADS and ATTN_KEY_CHUNK == ATTN_Q_TILE
    xs = x[0]
    H = MLA_HEADS

    lane = jnp.arange(LANES)
    zero, one = jnp.zeros((LANES,), F32), jnp.ones((LANES,), F32)
    f_mla = ROPE_THETA ** (-jnp.arange(0, MLA_ROPE_DIM, 2, dtype=F32) / MLA_ROPE_DIM)
    hr = MLA_ROPE_DIM // 2
    sign = jnp.where(lane % MLA_ROPE_DIM < hr, -1.0, 1.0).astype(F32)
    mla_tables = _rotary_factors(S, PROJ_ROW_TILE, jnp.tile(f_mla, LANES // hr), [(one, zero), (zero, sign)])
    f_moba = ROPE_THETA ** (-jnp.arange(0, MOBA_ROT_DIM, 2, dtype=F32) / MOBA_ROT_DIM)
    hr = MOBA_ROT_DIM // 2
    rot = lane < MOBA_ROT_DIM
    first = (lane < hr).astype(F32)
    second = (rot & (lane >= hr)).astype(F32)
    moba_tables = _rotary_factors(S, PROJ_ROW_TILE, jnp.where(rot, jnp.tile(f_moba, LANES // hr), 0.0),
                                  [(one, zero), (zero, -first), (zero, second)])

    wd = mla_w_dqkv[0]
    w_kr = wd[:, MLA_Q_RANK + MLA_KV_RANK:]
    w_kr_sw = _swap_halves(w_kr, MLA_ROPE_DIM)
    wd_ext = jnp.concatenate([wd[:, :MLA_Q_RANK + MLA_KV_RANK], w_kr, w_kr, w_kr_sw, w_kr_sw], axis=1).astype(BF16)
    wq = mla_w_uq[0].reshape(MLA_Q_RANK, H, MLA_QK_DIM)
    wq_nope = wq[:, :, :MLA_NOPE_DIM].reshape(MLA_Q_RANK, H * MLA_NOPE_DIM)
    wq_rope = wq[:, :, MLA_NOPE_DIM:].reshape(MLA_Q_RANK, H * MLA_ROPE_DIM)
    wq_ext = jnp.concatenate([wq_nope, wq_rope, _swap_halves(wq_rope, MLA_ROPE_DIM)], axis=1).astype(BF16)

    qt, k, vt = _mla_proj(xs, wd_ext, mla_q_norm[0][None, :], wq_ext, mla_kv_norm[0][None, :],
                          mla_w_ukv[0].astype(BF16), *mla_tables)
    w_in, w_out = ffn_w_in.astype(BF16), ffn_w_out.astype(BF16)
    norms = [p[:, None, :] for p in (ln_mix_g, ln_mix_b, ln_ffn_g, ln_ffn_b)]

    a = _attention(qt, k, vt, name="mla_attn")
    x1 = _post(xs, a, mla_w_o.astype(BF16), w_in, w_out, *norms, mixer_layer=0, layer=0)

    qt, k, vt = _moba_proj(x1, moba_w_qkv[0].astype(BF16), *moba_tables)
    a = _attention(qt, k, vt, name="moba_attn")
    x2 = _post(x1, a, moba_w_o.astype(BF16), w_in, w_out, *norms, mixer_layer=0, layer=1)
    return x2[None]
```
